```python
import jax, jax.numpy as jnp
from jax import lax
import numpy as np

D_MODEL = 1024
BATCH = 32
SEQ = 256
DEPTH = 4
DEC_BATCH = 4
DEC_SEQ = 4096
PAST_LEN = 256

GRID_W = 64
N_EVEN = (DEPTH + 1) // 2
N_ODD = DEPTH // 2
EPS = 1e-6
NEG_INF = -1e30
A_HEADS = 8
A_KV_HEADS = 2
GQA_GROUP = A_HEADS // A_KV_HEADS
HEAD_DIM = 64
WINDOW = 128
BLOCK = 128
ROPE_BASE = 10000.0
B_HEADS = 4
B_DK = 64
B_DV = 128
GATE_RANK = 16
GATE_NORMALIZER = 16.0
GLA_CHUNK = 16
D_RNN = D_MODEL
RG_BLOCKS = 16
RG_BW = D_RNN // RG_BLOCKS
RG_C = 8.0
CONV_W = 4
D_FF = 2816
FFN_CONV_W = 3
A_Q = A_HEADS * HEAD_DIM
A_KV = A_KV_HEADS * HEAD_DIM
B_QK = B_HEADS * B_DK
B_V = B_HEADS * B_DV
EVEN_IN = A_Q + 2 * A_KV + 2 * B_QK + 2 * B_V + 2 * GATE_RANK
MIX_OUT = A_Q + B_V

kernel_name = 'hybrid_swa_gla_rglru_prefix_diffusion_step'


def _rmsnorm(x, g):
    xf = x.astype(jnp.float32)
    y = xf * lax.rsqrt(jnp.mean(xf * xf, axis=-1, keepdims=True) + EPS)
    return (y * g.astype(jnp.float32)).astype(x.dtype)


def _modulation(cond, w_ada, b_ada):
    m = jax.nn.silu(cond) @ w_ada + b_ada
    return jnp.split(m[:, None, :], 6, axis=-1)


def _modulate(x, g, shift, scale):
    return _rmsnorm(x, g) * (1.0 + scale) + shift


def _dwconv(x, w, b):
    width = w.shape[0]
    left = width // 2
    t_len = x.shape[1]
    xp = jnp.pad(x, ((0, 0), (left, width - 1 - left), (0, 0)))
    out = xp[:, 0:t_len] * w[0]
    for k in range(1, width):
        out = out + xp[:, k:k + t_len] * w[k]
    return out + b


def _axial_rope_tables(t_len, dtype):
    rows = t_len // GRID_W
    row = jnp.repeat(jnp.arange(rows, dtype=jnp.float32), GRID_W)
    col = jnp.tile(jnp.arange(GRID_W, dtype=jnp.float32), rows)
    n_freq = HEAD_DIM // 4
    inv_freq = jnp.power(ROPE_BASE, -jnp.arange(n_freq, dtype=jnp.float32) / n_freq)
    ang_r = row[:, None, None] * inv_freq
    ang_c = col[:, None, None] * inv_freq
    return (jnp.cos(ang_r).astype(dtype), jnp.sin(ang_r).astype(dtype),
            jnp.cos(ang_c).astype(dtype), jnp.sin(ang_c).astype(dtype))


def _rot_half(t, cos, sin):
    n = t.shape[-1] // 2
    t1, t2 = t[..., :n], t[..., n:]
    return jnp.concatenate([t1 * cos - t2 * sin, t1 * sin + t2 * cos], axis=-1)


def _axial_rope(x, cos_r, sin_r, cos_c, sin_c):
    half = HEAD_DIM // 2
    return jnp.concatenate([_rot_half(x[..., :half], cos_r, sin_r),
                            _rot_half(x[..., half:], cos_c, sin_c)], axis=-1)


def _attn_context(q, k, v, sink):
    bsz, s_len = q.shape[0], q.shape[1]
    nq = s_len // BLOCK
    qb = (q * HEAD_DIM ** -0.5).reshape(bsz, nq, BLOCK, A_KV_HEADS, GQA_GROUP, HEAD_DIM).transpose(1, 0, 2, 3, 4, 5)
    sink_l = sink.astype(jnp.float32).reshape(A_KV_HEADS, GQA_GROUP)[None, :, :, None, None]

    def one_block(qblk):
        s = jnp.einsum('bqkgd,bskd->bkgqs', qblk, k, preferred_element_type=jnp.float32)
        logits = jnp.concatenate([s, jnp.broadcast_to(sink_l, s.shape[:-1] + (1,))], axis=-1)
        p = jax.nn.softmax(logits, axis=-1)[..., :-1].astype(v.dtype)
        return jnp.einsum('bkgqs,bskd->bqkgd', p, v)

    o = lax.map(one_block, qb)
    return o.transpose(1, 0, 2, 3, 4, 5).reshape(bsz, s_len, A_Q)


def _attn_latent(q, k, v, k_ctx, v_ctx, sink):
    bsz, t_len = q.shape[0], q.shape[1]
    nb = t_len // BLOCK
    qb = (q * HEAD_DIM ** -0.5).reshape(bsz, nb, BLOCK, A_KV_HEADS, GQA_GROUP, HEAD_DIM).transpose(1, 0, 2, 3, 4, 5)

    def neighbours(t):
        tb = jnp.pad(t, ((0, 0), (BLOCK, BLOCK), (0, 0), (0, 0))).reshape(bsz, nb + 2, BLOCK, A_KV_HEADS, HEAD_DIM)
        win = jnp.concatenate([tb[:, :-2], tb[:, 1:-1], tb[:, 2:]], axis=2)
        return win.transpose(1, 0, 2, 3, 4)

    kw, vw = neighbours(k), neighbours(v)
    qi = jnp.arange(BLOCK)[:, None]
    kj = jnp.arange(3 * BLOCK)[None, :]
    rel = kj - BLOCK - qi
    kpos = (jnp.arange(nb)[:, None, None] - 1) * BLOCK + kj[None]
    valid = (jnp.abs(rel) <= WINDOW)[None] & (kpos >= 0) & (kpos < t_len)
    sink_l = sink.astype(jnp.float32).reshape(A_KV_HEADS, GQA_GROUP)[None, :, :, None, None]

    def one_block(args):
        qblk, kblk, vblk, ok = args
        s_w = jnp.einsum('bqkgd,bskd->bkgqs', qblk, kblk, preferred_element_type=jnp.float32)
        s_w = jnp.where(ok[None, None, None], s_w, NEG_INF)
        s_c = jnp.einsum('bqkgd,bpkd->bkgqp', qblk, k_ctx, preferred_element_type=jnp.float32)
        snk = jnp.broadcast_to(sink_l, s_w.shape[:-1] + (1,))
        p = jax.nn.softmax(jnp.concatenate([s_w, s_c, snk], axis=-1), axis=-1).astype(v.dtype)
        return (jnp.einsum('bkgqs,bskd->bqkgd', p[..., :3 * BLOCK], vblk)
                + jnp.einsum('bkgqp,bpkd->bqkgd', p[..., 3 * BLOCK:-1], v_ctx))

    o = lax.map(one_block, (qb, kw, vw, valid))
    return o.transpose(1, 0, 2, 3, 4, 5).reshape(bsz, t_len, A_Q)


def _gla_chunked(q, k, v, log_a, s0):
    bsz, t_len, nh, dk = q.shape
    dv = v.shape[-1]
    n = t_len // GLA_CHUNK

    def chunks(t):
        return t.reshape(bsz, n, GLA_CHUNK, nh, t.shape[-1]).transpose(1, 0, 3, 2, 4).astype(jnp.float32)

    qc, kc, vc, lc = chunks(q), chunks(k), chunks(v), chunks(log_a)
    b = jnp.cumsum(lc, axis=3)
    causal = jnp.tril(jnp.ones((GLA_CHUNK, GLA_CHUNK), dtype=bool))[:, :, None]
    rel = b[..., :, None, :] - b[..., None, :, :]
    decay = jnp.exp(jnp.where(causal, rel, NEG_INF))
    scores = jnp.einsum('nbhid,nbhijd,nbhjd->nbhij', qc, decay, kc)
    o_intra = jnp.einsum('nbhij,nbhjv->nbhiv', scores, vc)
    b_last = b[..., -1:, :]
    q_in = qc * jnp.exp(b)
    k_in = kc * jnp.exp(b_last - b)
    a_chunk = jnp.exp(b_last[..., 0, :])

    def step(s, xs):
        qi, ki, vi, ai = xs
        o = jnp.einsum('bhcd,bhdv->bhcv', qi, s)
        s = ai[..., None] * s + jnp.einsum('bhcd,bhcv->bhdv', ki, vi)
        return s, o

    s_fin, o_inter = lax.scan(step, s0.astype(jnp.float32), (q_in, k_in, vc, a_chunk))
    o = (o_intra + o_inter).transpose(1, 0, 3, 2, 4).reshape(bsz, t_len, nh, dv)
    return o.astype(v.dtype), s_fin.astype(v.dtype)


def _gla_bidir(q, k, v, la_f, la_b, s0_f, s0_b):
    o_f, s_f = _gla_chunked(q, k, v, la_f, s0_f)
    o_b, s_b = _gla_chunked(q[:, ::-1], k[:, ::-1], v[:, ::-1], la_b[:, ::-1], s0_b)
    return o_f + o_b[:, ::-1], s_f, s_b


def _even_project(h, w_in, w_gate_f, b_gate_f, w_gate_b, b_gate_b):
    bsz, t_len = h.shape[0], h.shape[1]
    sizes = (A_Q, A_KV, A_KV, B_QK, B_QK, B_V, B_V, GATE_RANK, GATE_RANK)
    cuts = [sum(sizes[:n]) for n in range(1, len(sizes))]
    q_a, k_a, v_a, q_b, k_b, v_b, g_b, r_f, r_b = jnp.split(h @ w_in, cuts, axis=-1)
    q_a = q_a.reshape(bsz, t_len, A_HEADS, HEAD_DIM)
    k_a = k_a.reshape(bsz, t_len, A_KV_HEADS, HEAD_DIM)
    v_a = v_a.reshape(bsz, t_len, A_KV_HEADS, HEAD_DIM)
    q_b = q_b.reshape(bsz, t_len, B_HEADS, B_DK) * (B_DK ** -0.5)
    k_b = k_b.reshape(bsz, t_len, B_HEADS, B_DK)
    v_b = v_b.reshape(bsz, t_len, B_HEADS, B_DV)
    la_f = jax.nn.log_sigmoid((r_f @ w_gate_f + b_gate_f).astype(jnp.float32)).reshape(bsz, t_len, B_HEADS, B_DK) / GATE_NORMALIZER
    la_b = jax.nn.log_sigmoid((r_b @ w_gate_b + b_gate_b).astype(jnp.float32)).reshape(bsz, t_len, B_HEADS, B_DK) / GATE_NORMALIZER
    return q_a, k_a, v_a, q_b, k_b, v_b, g_b, la_f, la_b


def _even_output(o_a, o_b, g_b, gla_norm, w_out):
    bsz, t_len = o_a.shape[0], o_a.shape[1]
    o_b = _rmsnorm(o_b, gla_norm).reshape(bsz, t_len, B_V) * jax.nn.silu(g_b)
    return jnp.concatenate([o_a, o_b], axis=-1) @ w_out


def _even_context(h, w_in, sink, w_gate_f, b_gate_f, w_gate_b, b_gate_b, gla_norm, w_out):
    q_a, k_a, v_a, q_b, k_b, v_b, g_b, la_f, la_b = _even_project(h, w_in, w_gate_f, b_gate_f, w_gate_b, b_gate_b)
    o_a = _attn_context(q_a, k_a, v_a, sink)
    s0 = jnp.zeros((h.shape[0], B_HEADS, B_DK, B_DV), jnp.float32)
    o_b, s_f, s_b = _gla_bidir(q_b, k_b, v_b, la_f, la_b, s0, s0)
    return _even_output(o_a, o_b, g_b, gla_norm, w_out), k_a, v_a, s_f, s_b


def _even_latent(h, k_ctx, v_ctx, s0_f, s0_b, rope, w_in, sink, w_gate_f, b_gate_f, w_gate_b, b_gate_b, gla_norm, w_out):
    q_a, k_a, v_a, q_b, k_b, v_b, g_b, la_f, la_b = _even_project(h, w_in, w_gate_f, b_gate_f, w_gate_b, b_gate_b)
    q_a = _axial_rope(q_a, rope[0], rope[1], rope[2], rope[3])
    k_a = _axial_rope(k_a, rope[0], rope[1], rope[2], rope[3])
    o_a = _attn_latent(q_a, k_a, v_a, k_ctx, v_ctx, sink)
    o_b, _, _ = _gla_bidir(q_b, k_b, v_b, la_f, la_b, s0_f, s0_b)
    return _even_output(o_a, o_b, g_b, gla_norm, w_out)


def _block_diag(x, w, b):
    xb = x.reshape(x.shape[:-1] + (RG_BLOCKS, RG_BW))
    return jnp.einsum('btnd,nde->btne', xb, w).reshape(x.shape) + b


def _combine(left, right):
    a_l, u_l = left
    a_r, u_r = right
    return a_l * a_r, a_r * u_l + u_r


def _rglru(x, w_a, b_a, w_i, b_i, lam, h0):
    xf = x.astype(jnp.float32)
    r = jax.nn.sigmoid(_block_diag(xf, w_a, b_a))
    i = jax.nn.sigmoid(_block_diag(xf, w_i, b_i))
    log_a = RG_C * r * jax.nn.log_sigmoid(lam.astype(jnp.float32))
    a = jnp.exp(log_a)
    u = jnp.sqrt(-jnp.expm1(2.0 * log_a)) * (i * xf)
    a_cum, u_cum = lax.associative_scan(_combine, (a, u), axis=1)
    h = a_cum * h0.astype(jnp.float32)[:, None, :] + u_cum
    return h.astype(x.dtype)


def _odd_mixer(h, h0_f, h0_b, w_in, conv_w, conv_b, w_a, b_a, w_i, b_i, lam, w_out):
    y, xr = jnp.split(h @ w_in, 2, axis=-1)
    xr = _dwconv(xr, conv_w, conv_b)
    h_f = _rglru(xr, w_a[0], b_a[0], w_i[0], b_i[0], lam[0], h0_f)
    h_b = _rglru(xr[:, ::-1], w_a[1], b_a[1], w_i[1], b_i[1], lam[1], h0_b)[:, ::-1]
    out = (jax.nn.gelu(y) * (h_f + h_b)) @ w_out
    return out, h_f[:, -1], h_b[:, 0]


def _conv_ffn(h, w_up, conv_w, conv_b, w_down):
    u = _dwconv(h @ w_up, conv_w, conv_b)
    gate, val = jnp.split(u, 2, axis=-1)
    return (jax.nn.silu(gate) * val) @ w_down


def setup_inputs(seed: int = 0) -> dict:
    key = jax.random.key(seed)
    ks = jax.random.split(key, 48)
    f32 = jnp.float32

    def nrm(i, shape, scale):
        return jax.random.normal(ks[i], shape, f32) * scale

    u = jax.random.uniform(ks[40], (N_ODD, 2, D_RNN), f32, 0.9, 0.999)
    s = u ** (1.0 / RG_C)
    lam = jnp.log(s) - jnp.log1p(-s)
    return {
        'x_prompt': nrm(0, (BATCH, SEQ, D_MODEL), 1.0),
        'x_sample': nrm(1, (DEC_BATCH, DEC_SEQ, D_MODEL), 1.0),
        'cache_attn_k': nrm(2, (DEC_BATCH, N_EVEN, PAST_LEN, A_KV_HEADS, HEAD_DIM), 1.0),
        'cache_attn_v': nrm(3, (DEC_BATCH, N_EVEN, PAST_LEN, A_KV_HEADS, HEAD_DIM), 1.0),
        'state_gla': nrm(4, (DEC_BATCH, N_EVEN, 2, B_HEADS, B_DK, B_DV), 1.0),
        'state_rglru': nrm(5, (DEC_BATCH, N_ODD, 2, D_RNN), 0.5),
        'c': nrm(6, (DEC_BATCH, D_MODEL), 1.0),
        'c_ctx': nrm(7, (D_MODEL,), 1.0),
        'norm_mix': 1.0 + nrm(8, (DEPTH, D_MODEL), 0.02),
        'norm_ffn': 1.0 + nrm(9, (DEPTH, D_MODEL), 0.02),
        'w_ada': nrm(10, (DEPTH, D_MODEL, 6 * D_MODEL), 0.2 * D_MODEL ** -0.5),
        'b_ada': nrm(11, (DEPTH, 6 * D_MODEL), 0.01),
        'ev_w_in': nrm(12, (N_EVEN, D_MODEL, EVEN_IN), D_MODEL ** -0.5),
        'ev_sink': nrm(13, (N_EVEN, A_HEADS), 1.0),
        'ev_w_gate_f': nrm(14, (N_EVEN, GATE_RANK, B_QK), GATE_RANK ** -0.5),
        'ev_b_gate_f': nrm(15, (N_EVEN, B_QK), 0.1),
        'ev_w_gate_b': nrm(16, (N_EVEN, GATE_RANK, B_QK), GATE_RANK ** -0.5),
        'ev_b_gate_b': nrm(17, (N_EVEN, B_QK), 0.1),
        'ev_gla_norm': 1.0 + nrm(18, (N_EVEN, B_DV), 0.02),
        'ev_w_out': nrm(19, (N_EVEN, MIX_OUT, D_MODEL), MIX_OUT ** -0.5),
        'od_w_in': nrm(20, (N_ODD, D_MODEL, 2 * D_RNN), D_MODEL ** -0.5),
        'od_conv_w': nrm(21, (N_ODD, CONV_W, D_RNN), CONV_W ** -0.5),
        'od_conv_b': nrm(22, (N_ODD, D_RNN), 0.01),
        'od_w_a': nrm(23, (N_ODD, 2, RG_BLOCKS, RG_BW, RG_BW), RG_BW ** -0.5),
        'od_b_a': nrm(24, (N_ODD, 2, D_RNN), 0.01),
        'od_w_i': nrm(25, (N_ODD, 2, RG_BLOCKS, RG_BW, RG_BW), RG_BW ** -0.5),
        'od_b_i': nrm(26, (N_ODD, 2, D_RNN), 0.01),
        'od_lambda': lam,
        'od_w_out': nrm(27, (N_ODD, D_RNN, D_MODEL), D_RNN ** -0.5),
        'ffn_w_up': nrm(28, (DEPTH, D_MODEL, 2 * D_FF), D_MODEL ** -0.5),
        'ffn_conv_w': nrm(29, (DEPTH, FFN_CONV_W, 2 * D_FF), FFN_CONV_W ** -0.5),
        'ffn_conv_b': nrm(30, (DEPTH, 2 * D_FF), 0.01),
        'ffn_w_down': nrm(31, (DEPTH, D_FF, D_MODEL), D_FF ** -0.5),
        'final_norm': 1.0 + nrm(32, (D_MODEL,), 0.02),
    }


def reference(x_prompt, x_sample, cache_attn_k, cache_attn_v, state_gla, state_rglru, c, c_ctx,
              norm_mix, norm_ffn, w_ada, b_ada,
              ev_w_in, ev_sink, ev_w_gate_f, ev_b_gate_f, ev_w_gate_b, ev_b_gate_b, ev_gla_norm, ev_w_out,
              od_w_in, od_conv_w, od_conv_b, od_w_a, od_b_a, od_w_i, od_b_i, od_lambda, od_w_out,
              ffn_w_up, ffn_conv_w, ffn_conv_b, ffn_w_down, final_norm):
    rope = _axial_rope_tables(x_sample.shape[1], x_sample.dtype)
    xp, xs = x_prompt, x_sample
    new_k, new_v, new_gla, new_rg = [], [], [], []
    for layer in range(DEPTH):
        j = layer // 2
        sh_mp, sc_mp, g_mp, sh_fp, sc_fp, g_fp = _modulation(c_ctx[None, :], w_ada[layer], b_ada[layer])
        sh_ms, sc_ms, g_ms, sh_fs, sc_fs, g_fs = _modulation(c, w_ada[layer], b_ada[layer])
        hp = _modulate(xp, norm_mix[layer], sh_mp, sc_mp)
        hs = _modulate(xs, norm_mix[layer], sh_ms, sc_ms)
        if layer % 2 == 0:
            ew = (ev_w_in[j], ev_sink[j], ev_w_gate_f[j], ev_b_gate_f[j], ev_w_gate_b[j], ev_b_gate_b[j],
                  ev_gla_norm[j], ev_w_out[j])
            out_p, k_c, v_c, s_f, s_b = _even_context(hp, *ew)
            out_s = _even_latent(hs, cache_attn_k[:, j], cache_attn_v[:, j], state_gla[:, j, 0], state_gla[:, j, 1],
                                 rope, *ew)
            new_k.append(k_c)
            new_v.append(v_c)
            new_gla.append(jnp.stack([s_f, s_b], axis=1))
        else:
            ow = (od_w_in[j], od_conv_w[j], od_conv_b[j], od_w_a[j], od_b_a[j], od_w_i[j], od_b_i[j],
                  od_lambda[j], od_w_out[j])
            h0 = jnp.zeros((xp.shape[0], D_RNN), xp.dtype)
            out_p, r_f, r_b = _odd_mixer(hp, h0, h0, *ow)
            out_s, _, _ = _odd_mixer(hs, state_rglru[:, j, 0], state_rglru[:, j, 1], *ow)
            new_rg.append(jnp.stack([r_f, r_b], axis=1))
        xp = xp + g_mp * out_p
        xs = xs + g_ms * out_s
        hp = _modulate(xp, norm_ffn[layer], sh_fp, sc_fp)
        hs = _modulate(xs, norm_ffn[layer], sh_fs, sc_fs)
        xp = xp + g_fp * _conv_ffn(hp, ffn_w_up[layer], ffn_conv_w[layer], ffn_conv_b[layer], ffn_w_down[layer])
        xs = xs + g_fs * _conv_ffn(hs, ffn_w_up[layer], ffn_conv_w[layer], ffn_conv_b[layer], ffn_w_down[layer])
    y_prompt = _rmsnorm(xp, final_norm)
    y_sample = _rmsnorm(xs, final_norm)
    new_attn_k = jnp.stack(new_k, axis=1)
    new_attn_v = jnp.stack(new_v, axis=1)
    new_state_gla = jnp.stack(new_gla, axis=1)
    new_state_rglru = jnp.stack(new_rg, axis=1)
    return (y_prompt, y_sample, new_attn_k, new_attn_v, new_state_gla, new_state_rglru)
```

```python
import functools

import jax
import jax.numpy as jnp
import numpy as np
from jax import lax
from jax.experimental import pallas as pl
from jax.experimental.pallas import tpu as pltpu

F32 = jnp.float32
BF16 = jnp.bfloat16

EPS = 1e-6
NEG_INF = -1e30
GRID_W = 64
A_HEADS = 8
A_KV_HEADS = 2
HEAD_DIM = 64
WINDOW = 128
ATT_BLOCK = 128
ROPE_BASE = 10000.0
B_HEADS = 4
B_DK = 64
B_DV = 128
GATE_RANK = 16
GATE_NORMALIZER = 16.0
RG_BLOCKS = 16
RG_C = 8.0
CONV_W = 4
FFN_CONV_W = 3

LANES = 128
SUBLANES = 8
HALO = SUBLANES
ROW_TILE = 512
GLA_CHUNK = 128
SCAN_TILE = 256
SCAN_COLS = 512
FF_CHUNK = 256
VMEM_LIMIT = 56 * 1024 * 1024


def _cparams(n_axes):
    return pltpu.CompilerParams(dimension_semantics=("arbitrary",) * n_axes,
                                vmem_limit_bytes=VMEM_LIMIT)


def _resident(shape):
    nd = len(shape)
    return pl.BlockSpec(shape, lambda *_: (0,) * nd, pipeline_mode=pl.Buffered(1))


def _silu(x):
    return x * jax.nn.sigmoid(x)


def _rms(x, g):
    return x * lax.rsqrt(jnp.mean(x * x, axis=-1, keepdims=True) + EPS) * g


def _dot(a, b):
    return jnp.dot(a, b, preferred_element_type=F32)


def _dot_nt(a, b):
    return lax.dot_general(a, b, (((1,), (1,)), ((), ())), preferred_element_type=F32)


def _dot_tn(a, b):
    return lax.dot_general(a, b, (((0,), (0,)), ((), ())), preferred_element_type=F32)


def _ada_kernel(c_ref, w_ref, b_ref, o_ref):
    s = _silu(c_ref[...])
    o_ref[...] = _dot(s.astype(BF16), w_ref[...].astype(BF16)) + b_ref[...]


def _ada(cond, w_ada, b_ada):
    depth, d, d6 = w_ada.shape
    nb = 4
    cb = d6 // nb
    rows = cond.shape[0]
    out = pl.pallas_call(
        _ada_kernel,
        grid=(depth, nb),
        in_specs=[pl.BlockSpec((rows, d), lambda l, j: (0, 0)),
                  pl.BlockSpec((None, d, cb), lambda l, j: (l, 0, j)),
                  pl.BlockSpec((None, 1, cb), lambda l, j: (l, 0, j))],
        out_specs=pl.BlockSpec((None, rows, cb), lambda l, j: (l, 0, j)),
        out_shape=jax.ShapeDtypeStruct((depth, rows, d6), F32),
        compiler_params=_cparams(2),
        name="ada",
    )(cond, w_ada, b_ada.reshape(depth, 1, d6))
    return out.reshape(depth, rows, 6, d)


class _Geom:
    def __init__(self, n_ctx_seq, ctx_len, n_lat_seq, lat_len, tile):
        assert ctx_len % tile == 0 or tile % ctx_len == 0
        assert lat_len % tile == 0
        self.n_ctx_seq, self.ctx_len, self.n_lat_seq, self.lat_len = n_ctx_seq, ctx_len, n_lat_seq, lat_len
        self.tile = tile
        self.n_ctx = n_ctx_seq * ctx_len
        self.n_lat = n_lat_seq * lat_len
        self.n = self.n_ctx + self.n_lat
        assert self.n_ctx % tile == 0
        self.ctx_tiles = self.n_ctx // tile
        self.lat_tiles_per_seq = lat_len // tile
        self.tiles = self.n // tile

    def mod_row(self, i):
        return jnp.where(i < self.ctx_tiles, 0, 1 + (i - self.ctx_tiles) // self.lat_tiles_per_seq)

    def lat_tile(self, i):
        return jnp.where(i < self.ctx_tiles, 0, (i - self.ctx_tiles) % self.lat_tiles_per_seq)

    def seq_pos(self, i, rows, offset=0):
        r = lax.broadcasted_iota(jnp.int32, (rows, 1), 0) + (i * self.tile + offset)
        is_ctx = i < self.ctx_tiles
        seq_len = jnp.where(is_ctx, self.ctx_len, self.lat_len)
        base = jnp.where(is_ctx, 0, self.n_ctx)
        return (r - base) % seq_len, seq_len


def _halo_specs(geom, d):
    t = geom.tile
    per = t // HALO
    last = geom.n // HALO - 1
    return [pl.BlockSpec((HALO, d), lambda i: (jnp.maximum(i * per - 1, 0), 0)),
            pl.BlockSpec((t, d), lambda i: (i, 0)),
            pl.BlockSpec((HALO, d), lambda i: (jnp.minimum((i + 1) * per, last), 0))]


def _mod_spec(geom, layer, d):
    return pl.BlockSpec((None, None, 6, d), lambda i: (layer, geom.mod_row(i), 0, 0))


def _row_spec(geom, width, col=0):
    return pl.BlockSpec((geom.tile, width), lambda i: (i, col))


def _shift_rows(x_ext, shift, rows):
    if shift == 0:
        return x_ext[HALO:HALO + rows]
    return pltpu.roll(x_ext, shift % x_ext.shape[0], 0)[HALO:HALO + rows]


def _rope_apply(t, cos, sin):
    lane = lax.broadcasted_iota(jnp.int32, t.shape, 1)
    swapped = jnp.where((lane & 16) == 0, pltpu.roll(t, LANES - 16, 1), pltpu.roll(t, 16, 1))
    return t * cos + swapped * sin


def _even_in_kernel(x_ref, mod_ref, g_ref, w_ref, wg_ref, bg_ref, cos_ref, sin_ref,
                    qa_ref, ka_ref, va_ref, qkb_ref, vb_ref, gb_ref, la_ref, *, geom):
    i = pl.program_id(0)
    h = _rms(x_ref[...], g_ref[...]) * (1.0 + mod_ref[1:2, :]) + mod_ref[0:1, :]
    p = _dot(h.astype(BF16), w_ref[...])
    aq = A_HEADS * HEAD_DIM
    akv = A_KV_HEADS * HEAD_DIM
    bqk = B_HEADS * B_DK
    bv = B_HEADS * B_DV
    o = 0
    q_a = p[:, o:o + aq] * (HEAD_DIM ** -0.5); o += aq
    k_a = p[:, o:o + akv]; o += akv
    va_ref[...] = p[:, o:o + akv]; o += akv
    qkb_ref[:, 0:bqk] = p[:, o:o + bqk] * (B_DK ** -0.5); o += bqk
    qkb_ref[:, bqk:2 * bqk] = p[:, o:o + bqk]; o += bqk
    vb_ref[...] = p[:, o:o + bv]; o += bv
    gb_ref[...] = p[:, o:o + bv]; o += bv
    r = p[:, o:o + LANES]
    z = _dot(r.astype(BF16), wg_ref[...]) + bg_ref[...]
    la_ref[...] = jax.nn.log_sigmoid(z) * (1.0 / GATE_NORMALIZER)

    is_lat = i >= geom.ctx_tiles
    cos, sin = cos_ref[...], sin_ref[...]
    for j in range(aq // LANES):
        blk = q_a[:, j * LANES:(j + 1) * LANES]
        qa_ref[:, j * LANES:(j + 1) * LANES] = jnp.where(is_lat, _rope_apply(blk, cos, sin), blk)
    for j in range(akv // LANES):
        blk = k_a[:, j * LANES:(j + 1) * LANES]
        ka_ref[:, j * LANES:(j + 1) * LANES] = jnp.where(is_lat, _rope_apply(blk, cos, sin), blk)


def _even_in(x, mod, layer, g, w_in_p, wg, bg, cos_t, sin_t, geom):
    n, d = x.shape
    t = geom.tile
    aq = A_HEADS * HEAD_DIM
    akv = A_KV_HEADS * HEAD_DIM
    bqk = B_HEADS * B_DK
    bv = B_HEADS * B_DV
    widths = (aq, akv, akv, 2 * bqk, bv, bv, 2 * bqk)
    return pl.pallas_call(
        functools.partial(_even_in_kernel, geom=geom),
        grid=(geom.tiles,),
        in_specs=[_row_spec(geom, d), _mod_spec(geom, layer, d), _resident(g.shape),
                  _resident(w_in_p.shape), _resident(wg.shape), _resident(bg.shape),
                  pl.BlockSpec((t, LANES), lambda i: (geom.lat_tile(i), 0)),
                  pl.BlockSpec((t, LANES), lambda i: (geom.lat_tile(i), 0))],
        out_specs=[_row_spec(geom, w) for w in widths],
        out_shape=[jax.ShapeDtypeStruct((n, w), F32) for w in widths],
        compiler_params=_cparams(1),
        name="even_in",
    )(x, mod, g, w_in_p, wg, bg, cos_t, sin_t)


def _attend(q, k, v, mask, sink_ref, o_ref):
    lane = lax.broadcasted_iota(jnp.int32, k.shape, 1)
    lo = lane < HEAD_DIM
    k_sw = pltpu.roll(k, HEAD_DIM, 1)
    v_sw = pltpu.roll(v, HEAD_DIM, 1)
    zero = jnp.zeros_like(k)
    kk = [[jnp.where(lo, k, zero).astype(BF16), jnp.where(lo, zero, k_sw).astype(BF16)],
          [jnp.where(lo, k_sw, zero).astype(BF16), jnp.where(lo, zero, k).astype(BF16)]]
    vv = [[jnp.where(lo, v, zero).astype(BF16), jnp.where(lo, zero, v_sw).astype(BF16)],
          [jnp.where(lo, v_sw, zero).astype(BF16), jnp.where(lo, zero, v).astype(BF16)]]
    group = A_HEADS // A_KV_HEADS
    for pair in range(A_HEADS // 2):
        qp = q[:, pair * LANES:(pair + 1) * LANES].astype(BF16)
        acc = None
        for half in range(2):
            head = 2 * pair + half
            kv = head // group
            s = _dot_nt(qp, kk[kv][half])
            if mask is not None:
                s = jnp.where(mask, s, NEG_INF)
            snk = sink_ref[head]
            m = jnp.maximum(jnp.max(s, axis=1, keepdims=True), snk)
            e = jnp.exp(s - m)
            l = jnp.sum(e, axis=1, keepdims=True) + jnp.exp(snk - m)
            o = _dot((e / l).astype(BF16), vv[kv][half])
            acc = o if acc is None else acc + o
        o_ref[:, pair * LANES:(pair + 1) * LANES] = acc


def _attn_ctx_kernel(sink_ref, q_ref, k_ref, v_ref, o_ref):
    _attend(q_ref[...], k_ref[...], v_ref[...], None, sink_ref, o_ref)


def _attn_lat_kernel(sink_ref, q_ref, kp_ref, kc_ref, kn_ref, vp_ref, vc_ref, vn_ref, kx_ref, vx_ref, o_ref,
                     *, n_blocks):
    i = pl.program_id(1)
    blk = ATT_BLOCK
    k = jnp.concatenate([kp_ref[...], kc_ref[...], kn_ref[...], kx_ref[...]], axis=0)
    v = jnp.concatenate([vp_ref[...], vc_ref[...], vn_ref[...], vx_ref[...]], axis=0)
    nk = k.shape[0]
    qi = lax.broadcasted_iota(jnp.int32, (blk, nk), 0)
    kj = lax.broadcasted_iota(jnp.int32, (blk, nk), 1)
    rel = kj - blk - qi
    in_win = (jnp.abs(rel) <= WINDOW) & (kj < 3 * blk)
    in_win = in_win & ((kj >= blk) | (i > 0)) & ((kj < 2 * blk) | (i < n_blocks - 1))
    mask = in_win | (kj >= 3 * blk)
    _attend(q_ref[...], k, v, mask, sink_ref, o_ref)


def _attention(qa, ka, va, sink, cache_k, cache_v, geom):
    aq = qa.shape[1]
    akv = ka.shape[1]
    smem = pl.BlockSpec(memory_space=pltpu.SMEM)
    cl = geom.ctx_len
    o_ctx = pl.pallas_call(
        _attn_ctx_kernel,
        grid=(geom.n_ctx_seq,),
        in_specs=[smem,
                  pl.BlockSpec((cl, aq), lambda b: (b, 0)),
                  pl.BlockSpec((cl, akv), lambda b: (b, 0)),
                  pl.BlockSpec((cl, akv), lambda b: (b, 0))],
        out_specs=pl.BlockSpec((cl, aq), lambda b: (b, 0)),
        out_shape=jax.ShapeDtypeStruct((geom.n_ctx, aq), F32),
        compiler_params=_cparams(1),
        name="attn_ctx",
    )(sink, qa, ka, va)

    blk = ATT_BLOCK
    nb = geom.lat_len // blk
    base = geom.n_ctx // blk
    past = cache_k.shape[1]

    def cur(b, i):
        return (base + b * nb + i, 0)

    def prv(b, i):
        return (base + b * nb + jnp.maximum(i - 1, 0), 0)

    def nxt(b, i):
        return (base + b * nb + jnp.minimum(i + 1, nb - 1), 0)

    kv_spec = lambda f: pl.BlockSpec((blk, akv), f)
    cache_spec = pl.BlockSpec((None, past, akv), lambda b, i: (b, 0, 0))
    o_lat = pl.pallas_call(
        functools.partial(_attn_lat_kernel, n_blocks=nb),
        grid=(geom.n_lat_seq, nb),
        in_specs=[smem, pl.BlockSpec((blk, aq), cur),
                  kv_spec(prv), kv_spec(cur), kv_spec(nxt),
                  kv_spec(prv), kv_spec(cur), kv_spec(nxt),
                  cache_spec, cache_spec],
        out_specs=pl.BlockSpec((blk, aq), lambda b, i: (b * nb + i, 0)),
        out_shape=jax.ShapeDtypeStruct((geom.n_lat, aq), F32),
        compiler_params=_cparams(2),
        name="attn_lat",
    )(sink, qa, ka, ka, ka, va, va, va, cache_k, cache_v)
    return o_ctx, o_lat


def _gla_constants(c):
    levels = int(np.log2(c))
    r = np.arange(c)[:, None]
    t = np.arange(c)[None, :]
    secs = [(t <= r), (t > r)]
    is_q = []
    lv = np.full((c, c), -1, np.int32)
    lv[np.arange(c), np.arange(c)] = levels
    for lvl in range(levels):
        s = c >> (lvl + 1)
        pos = r % (2 * s)
        m = r - pos + s - 1
        q_side = pos >= s
        w = np.where(q_side, (t > m) & (t <= r), (t > r) & (t <= m))
        secs.append(w)
        is_q.append(np.broadcast_to(q_side, (c, 1)).astype(np.float32))
        i_idx = np.arange(c)[:, None]
        j_idx = np.arange(c)[None, :]
        same_pair = (i_idx // (2 * s)) == (j_idx // (2 * s))
        i_q = (i_idx % (2 * s)) >= s
        j_k = (j_idx % (2 * s)) < s
        lv[same_pair & i_q & j_k] = lvl
    secs.append(np.ones((SUBLANES, c), bool))
    w_f = np.concatenate([np.asarray(x, np.float32) for x in secs], axis=0)
    isq_f = np.concatenate(is_q, axis=1)
    w_b = np.concatenate([np.asarray(x, np.float32)[::-1, ::-1] if x.shape[0] == c else np.asarray(x, np.float32)
                          for x in secs], axis=0)
    isq_b = isq_f[::-1]
    lv_b = lv[::-1, ::-1]
    w_all = jnp.asarray(np.stack([w_f, w_b]), BF16)
    isq = np.zeros((2, c, LANES), np.float32)
    isq[0, :, :levels] = isq_f
    isq[1, :, :levels] = isq_b
    return w_all, jnp.asarray(isq), jnp.asarray(np.stack([lv, lv_b]))


def _gla_kernel(qk_q_ref, qk_k_ref, v_ref, la_ref, s0_ref, w_ref, isq_ref, lv_ref,
                o_ref, sfin_ref, st_ref, *, first_last):
    d = pl.program_id(0)
    n = pl.program_id(2)
    c = GLA_CHUNK
    levels = int(np.log2(c))
    is_first, is_last = first_last(d, n)
    lane = lax.broadcasted_iota(jnp.int32, (c, LANES), 1)
    head_lo = lane < B_DK
    row2 = lax.broadcasted_iota(jnp.int32, (2 * B_DV, LANES), 0)
    lane2 = lax.broadcasted_iota(jnp.int32, (2 * B_DV, LANES), 1)
    diag_blocks = (row2 < B_DV) == (lane2 < B_DK)

    @pl.when(is_first)
    def _():
        s0 = jnp.concatenate([s0_ref[0], s0_ref[1]], axis=0)
        s0t = s0.T
        st_ref[...] = jnp.where(diag_blocks, jnp.concatenate([s0t, s0t], axis=0), 0.0)

    q = qk_q_ref[...]
    k = qk_k_ref[...]
    v = v_ref[...]
    la = la_ref[...]
    la_hi = la.astype(BF16)
    r1 = la - la_hi.astype(F32)
    la_mid = r1.astype(BF16)
    la_lo = (r1 - la_mid.astype(F32)).astype(BF16)
    w = w_ref[...]
    ex = (_dot(w, la_hi) + _dot(w, la_mid)) + _dot(w, la_lo)
    e_q = jnp.exp(ex[0:c])
    e_k = jnp.exp(ex[c:2 * c])
    a_chunk = jnp.exp(ex[(levels + 2) * c:(levels + 2) * c + 1])
    k_b = k.astype(BF16)
    v_b = v.astype(BF16)
    lv = lv_ref[...]
    isq = isq_ref[...]
    zero = jnp.zeros_like(q)

    q_heads = [jnp.where(head_lo, q, zero), jnp.where(head_lo, zero, q)]
    a = [jnp.where(lv == levels, _dot_nt(qh.astype(BF16), k_b), 0.0) for qh in q_heads]
    for lvl in range(levels):
        e = jnp.exp(ex[(2 + lvl) * c:(3 + lvl) * c])
        q_side = isq[:, lvl:lvl + 1] > 0.5
        kt = jnp.where(q_side, zero, k * e).astype(BF16)
        qe = jnp.where(q_side, q * e, zero)
        for hh in range(2):
            qt = jnp.where(head_lo if hh == 0 else ~head_lo, qe, zero).astype(BF16)
            a[hh] = jnp.where(lv == lvl, _dot_nt(qt, kt), a[hh])

    st = st_ref[...]
    o_inter = _dot_nt((q * e_q).astype(BF16), st.astype(BF16))
    for hh in range(2):
        o_intra = _dot(a[hh].astype(BF16), v_b[:, hh * B_DV:(hh + 1) * B_DV])
        o_ref[:, hh * B_DV:(hh + 1) * B_DV] = o_intra + o_inter[:, hh * B_DV:(hh + 1) * B_DV]
    upd = _dot_tn(v_b, (k * e_k).astype(BF16))
    st_new = jnp.where(diag_blocks, a_chunk * st + upd, 0.0)
    st_ref[...] = st_new

    @pl.when(is_last)
    def _():
        for hh in range(2):
            blk = st_new[hh * B_DV:(hh + 1) * B_DV, :].T
            sfin_ref[hh] = blk[hh * B_DK:(hh + 1) * B_DK, :]


def _gla(qkb, vb, la, s0_all, consts, geom):
    n = qkb.shape[0]
    c = GLA_CHUNK
    w_all, isq, lv = consts
    n_chunks = n // c
    ctx_chunks = geom.n_ctx // c
    cpc = geom.ctx_len // c
    cpl = geom.lat_len // c
    n_seq = geom.n_ctx_seq + geom.n_lat_seq
    pairs = B_HEADS // 2

    def chunk(d, nn):
        return jnp.where(d == 0, nn, n_chunks - 1 - nn)

    def seq_of(g):
        return jnp.where(g < ctx_chunks, g // cpc, geom.n_ctx_seq + (g - ctx_chunks) // cpl)

    def first_last(d, nn):
        g = chunk(d, nn)
        pos = jnp.where(g < ctx_chunks, g % cpc, (g - ctx_chunks) % cpl)
        per = jnp.where(g < ctx_chunks, cpc, cpl)
        at_start = pos == 0
        at_end = pos == per - 1
        return jnp.where(d == 0, at_start, at_end), jnp.where(d == 0, at_end, at_start)

    state_spec = pl.BlockSpec((None, None, 2, B_DK, B_DV), lambda d, p, nn: (seq_of(chunk(d, nn)), d, p, 0, 0))
    o, sfin = pl.pallas_call(
        functools.partial(_gla_kernel, first_last=first_last),
        grid=(2, pairs, n_chunks),
        in_specs=[pl.BlockSpec((c, LANES), lambda d, p, nn: (chunk(d, nn), p)),
                  pl.BlockSpec((c, LANES), lambda d, p, nn: (chunk(d, nn), pairs + p)),
                  pl.BlockSpec((c, 2 * B_DV), lambda d, p, nn: (chunk(d, nn), p)),
                  pl.BlockSpec((c, LANES), lambda d, p, nn: (chunk(d, nn), d * pairs + p)),
                  state_spec,
                  pl.BlockSpec((None,) + w_all.shape[1:], lambda d, p, nn: (d, 0, 0)),
                  pl.BlockSpec((None,) + isq.shape[1:], lambda d, p, nn: (d, 0, 0)),
                  pl.BlockSpec((None,) + lv.shape[1:], lambda d, p, nn: (d, 0, 0))],
        out_specs=[pl.BlockSpec((None, c, 2 * B_DV), lambda d, p, nn: (d, chunk(d, nn), p)),
                   state_spec],
        out_shape=[jax.ShapeDtypeStruct((2, n, B_HEADS * B_DV), F32),
                   jax.ShapeDtypeStruct((n_seq, 2, B_HEADS, B_DK, B_DV), F32)],
        scratch_shapes=[pltpu.VMEM((2 * B_DV, LANES), F32)],
        compiler_params=_cparams(3),
        name="gla",
    )(qkb, qkb, vb, la, s0_all, w_all, isq, lv)
    return o, sfin


def _even_out_kernel(x_ref, mod_ref, oa_ref, of_ref, ob_ref, gb_ref, gn_ref, w_ref, o_ref):
    ob = of_ref[...] + ob_ref[...]
    gb = gb_ref[...]
    gn = gn_ref[...]
    aq = oa_ref.shape[1]
    acc = _dot(oa_ref[...].astype(BF16), w_ref[0:aq, :])
    for hh in range(B_HEADS):
        blk = _rms(ob[:, hh * B_DV:(hh + 1) * B_DV], gn) * _silu(gb[:, hh * B_DV:(hh + 1) * B_DV])
        acc = acc + _dot(blk.astype(BF16), w_ref[aq + hh * B_DV:aq + (hh + 1) * B_DV, :])
    o_ref[...] = x_ref[...] + mod_ref[2:3, :] * acc


def _even_out(x, mod, layer, oa, o_gla, gb, gn, w_out, geom):
    n, d = x.shape
    t = geom.tile
    bv = gb.shape[1]
    return pl.pallas_call(
        _even_out_kernel,
        grid=(geom.tiles,),
        in_specs=[_row_spec(geom, d), _mod_spec(geom, layer, d), _row_spec(geom, oa.shape[1]),
                  pl.BlockSpec((None, t, bv), lambda i: (0, i, 0)),
                  pl.BlockSpec((None, t, bv), lambda i: (1, i, 0)),
                  _row_spec(geom, bv), _resident(gn.shape), _resident(w_out.shape)],
        out_specs=_row_spec(geom, d),
        out_shape=jax.ShapeDtypeStruct((n, d), F32),
        compiler_params=_cparams(1),
        name="even_out",
    )(x, mod, oa, o_gla, o_gla, gb, gn, w_out)


def _odd_in_kernel(xp_ref, x_ref, xn_ref, mod_ref, g_ref, wy_ref, wx_ref, cw_ref, cb_ref,
                   wg_ref, bg_ref, lam_ref, gy_ref, a_ref, u_ref, *, geom):
    i = pl.program_id(0)
    t = geom.tile
    x_ext = jnp.concatenate([xp_ref[...], x_ref[...], xn_ref[...]], axis=0)
    h_ext = _rms(x_ext, g_ref[...]) * (1.0 + mod_ref[1:2, :]) + mod_ref[0:1, :]
    y = _dot(h_ext[HALO:HALO + t].astype(BF16), wy_ref[...])
    gy_ref[...] = jax.nn.gelu(y)
    xr_ext = _dot(h_ext.astype(BF16), wx_ref[...])
    pos, seq_len = geom.seq_pos(i, t)
    left = CONV_W // 2
    xr = None
    for tap in range(CONV_W):
        off = tap - left
        term = _shift_rows(xr_ext, -off, t)
        if off < 0:
            term = jnp.where(pos >= -off, term, 0.0)
        elif off > 0:
            term = jnp.where(pos < seq_len - off, term, 0.0)
        term = term * cw_ref[tap:tap + 1, :]
        xr = term if xr is None else xr + term
    xr = xr + cb_ref[...]
    xr_b = xr.astype(BF16)
    d = xr.shape[1]
    grp = wg_ref.shape[2]
    n_grp = d // grp
    for dr in range(2):
        log_sig_lam = jax.nn.log_sigmoid(lam_ref[dr:dr + 1, :])
        for gi in range(n_grp):
            cols = slice(gi * grp, (gi + 1) * grp)
            ri = _dot(xr_b[:, cols], wg_ref[dr, gi]) + bg_ref[dr, gi]
            r_gate = jax.nn.sigmoid(ri[:, :grp])
            i_gate = jax.nn.sigmoid(ri[:, grp:])
            log_a = RG_C * r_gate * log_sig_lam[:, cols]
            a = jnp.exp(log_a)
            a_ref[dr, :, cols] = a
            u_ref[dr, :, cols] = jnp.sqrt(-jnp.tanh(log_a) * (a * a + 1.0)) * (i_gate * xr[:, cols])


def _odd_in(x, mod, layer, g, wy, wx, cw, cb, wg, bg, lam, geom):
    n, d = x.shape
    t = geom.tile
    d_rnn = wy.shape[1]
    return pl.pallas_call(
        functools.partial(_odd_in_kernel, geom=geom),
        grid=(geom.tiles,),
        in_specs=_halo_specs(geom, d) + [_mod_spec(geom, layer, d)] +
                 [_resident(a.shape) for a in (g, wy, wx, cw, cb, wg, bg, lam)],
        out_specs=[_row_spec(geom, d_rnn),
                   pl.BlockSpec((2, t, d_rnn), lambda i: (0, i, 0)),
                   pl.BlockSpec((2, t, d_rnn), lambda i: (0, i, 0))],
        out_shape=[jax.ShapeDtypeStruct((n, d_rnn), F32),
                   jax.ShapeDtypeStruct((2, n, d_rnn), F32),
                   jax.ShapeDtypeStruct((2, n, d_rnn), F32)],
        compiler_params=_cparams(1),
        name="odd_in",
    )(x, x, x, mod, g, wy, wx, cw, cb, wg, bg, lam)


def _scan_kernel(a_ref, u_ref, h0_ref, h_ref, carry_ref, *, first_of):
    d = pl.program_id(1)
    n = pl.program_id(2)
    ts = SCAN_TILE

    @pl.when(first_of(d, n))
    def _():
        carry_ref[...] = h0_ref[...]

    row = lax.broadcasted_iota(jnp.int32, a_ref.shape, 0)

    def scan(reverse):
        a = a_ref[...]
        u = u_ref[...]
        k = 1
        while k < ts:
            shift = ts - k if reverse else k
            ok = row < ts - k if reverse else row >= k
            a_s = jnp.where(ok, pltpu.roll(a, shift, 0), 1.0)
            u_s = jnp.where(ok, pltpu.roll(u, shift, 0), 0.0)
            u = a * u_s + u
            a = a * a_s
            k *= 2
        h = a * carry_ref[...] + u
        h_ref[...] = h
        carry_ref[...] = h[0:1, :] if reverse else h[ts - 1:ts, :]

    @pl.when(d == 0)
    def _():
        scan(False)

    @pl.when(d == 1)
    def _():
        scan(True)


def _rglru_scan(a, u, h0_all, geom):
    _, n, d = a.shape
    ts = SCAN_TILE
    cb = min(SCAN_COLS, d)
    n_tiles = n // ts
    ctx_tiles = geom.n_ctx // ts
    tpc = geom.ctx_len // ts
    tpl = geom.lat_len // ts

    def tile(dd, nn):
        return jnp.where(dd == 0, nn, n_tiles - 1 - nn)

    def seq_of(g):
        return jnp.where(g < ctx_tiles, g // tpc, geom.n_ctx_seq + (g - ctx_tiles) // tpl)

    def first_of(dd, nn):
        g = tile(dd, nn)
        pos = jnp.where(g < ctx_tiles, g % tpc, (g - ctx_tiles) % tpl)
        per = jnp.where(g < ctx_tiles, tpc, tpl)
        return jnp.where(dd == 0, pos == 0, pos == per - 1)

    blk = pl.BlockSpec((None, ts, cb), lambda j, dd, nn: (dd, tile(dd, nn), j))
    return pl.pallas_call(
        functools.partial(_scan_kernel, first_of=first_of),
        grid=(d // cb, 2, n_tiles),
        in_specs=[blk, blk,
                  pl.BlockSpec((None, None, 1, cb), lambda j, dd, nn: (seq_of(tile(dd, nn)), dd, 0, j))],
        out_specs=blk,
        out_shape=jax.ShapeDtypeStruct((2, n, d), F32),
        scratch_shapes=[pltpu.VMEM((1, cb), F32)],
        compiler_params=_cparams(3),
        name="rglru_scan",
    )(a, u, h0_all)


def _odd_out_kernel(x_ref, mod_ref, gy_ref, hf_ref, hb_ref, w_ref, o_ref):
    z = gy_ref[...] * (hf_ref[...] + hb_ref[...])
    o_ref[...] = x_ref[...] + mod_ref[2:3, :] * _dot(z.astype(BF16), w_ref[...])


def _odd_out(x, mod, layer, gy, h, w_out, geom):
    n, d = x.shape
    t = geom.tile
    dr = gy.shape[1]
    return pl.pallas_call(
        _odd_out_kernel,
        grid=(geom.tiles,),
        in_specs=[_row_spec(geom, d), _mod_spec(geom, layer, d), _row_spec(geom, dr),
                  pl.BlockSpec((None, t, dr), lambda i: (0, i, 0)),
                  pl.BlockSpec((None, t, dr), lambda i: (1, i, 0)),
                  _resident(w_out.shape)],
        out_specs=_row_spec(geom, d),
        out_shape=jax.ShapeDtypeStruct((n, d), F32),
        compiler_params=_cparams(1),
        name="odd_out",
    )(x, mod, gy, h, h, w_out)


def _ffn_kernel(xp_ref, x_ref, xn_ref, mod_ref, g_ref, wu_ref, cw_ref, cb_ref, wd_ref, fn_ref, o_ref,
                *, geom, final):
    i = pl.program_id(0)
    t = geom.tile
    x = x_ref[...]
    x_ext = jnp.concatenate([xp_ref[...], x, xn_ref[...]], axis=0)
    h_ext = (_rms(x_ext, g_ref[...]) * (1.0 + mod_ref[4:5, :]) + mod_ref[3:4, :]).astype(BF16)
    pos, seq_len = geom.seq_pos(i, t)
    has_prev = pos >= 1
    has_next = pos < seq_len - 1
    n_chunks = wd_ref.shape[0]
    acc = None
    for c in range(n_chunks):
        halves = []
        for part in range(2):
            u_ext = _dot(h_ext, wu_ref[part, c])
            cw = cw_ref[part, c]
            u = (jnp.where(has_prev, _shift_rows(u_ext, 1, t), 0.0) * cw[0:1, :]
                 + u_ext[HALO:HALO + t] * cw[1:2, :]
                 + jnp.where(has_next, _shift_rows(u_ext, -1, t), 0.0) * cw[2:3, :]
                 + cb_ref[part, c])
            halves.append(u)
        act = (_silu(halves[0]) * halves[1]).astype(BF16)
        part_out = _dot(act, wd_ref[c])
        acc = part_out if acc is None else acc + part_out
    y = x + mod_ref[5:6, :] * acc
    if final:
        y = _rms(y, fn_ref[...])
    o_ref[...] = y


def _ffn(x, mod, layer, g, wu, cw, cb, wd, fn, geom, final):
    n, d = x.shape
    return pl.pallas_call(
        functools.partial(_ffn_kernel, geom=geom, final=final),
        grid=(geom.tiles,),
        in_specs=_halo_specs(geom, d) + [_mod_spec(geom, layer, d)] +
                 [_resident(a.shape) for a in (g, wu, cw, cb, wd, fn)],
        out_specs=_row_spec(geom, d),
        out_shape=jax.ShapeDtypeStruct((n, d), F32),
        compiler_params=_cparams(1),
        name="ffn",
    )(x, x, x, mod, g, wu, cw, cb, wd, fn)


def _rope_tables(t_len):
    rows = t_len // GRID_W
    row = jnp.repeat(jnp.arange(rows, dtype=F32), GRID_W)
    col = jnp.tile(jnp.arange(GRID_W, dtype=F32), rows)
    n_freq = HEAD_DIM // 4
    inv_freq = jnp.power(ROPE_BASE, -jnp.arange(n_freq, dtype=F32) / n_freq)
    ang_r = row[:, None] * inv_freq
    ang_c = col[:, None] * inv_freq
    cos_h = jnp.concatenate([jnp.cos(ang_r)] * 2 + [jnp.cos(ang_c)] * 2, axis=1)
    sin_h = jnp.concatenate([-jnp.sin(ang_r), jnp.sin(ang_r), -jnp.sin(ang_c), jnp.sin(ang_c)], axis=1)
    reps = LANES // HEAD_DIM
    return jnp.tile(cos_h, (1, reps)), jnp.tile(sin_h, (1, reps))


def _block_diag_groups(w, grp):
    nb, bw, _ = w.shape
    per = grp // bw
    w = w.reshape(nb // per, per, bw, bw)
    eye = jnp.eye(per, dtype=w.dtype)
    return jnp.einsum('gpij,pq->gpiqj', w, eye).reshape(nb // per, grp, grp)


def kernel(x_prompt, x_sample, cache_attn_k, cache_attn_v, state_gla, state_rglru, c, c_ctx,
           norm_mix, norm_ffn, w_ada, b_ada,
           ev_w_in, ev_sink, ev_w_gate_f, ev_b_gate_f, ev_w_gate_b, ev_b_gate_b, ev_gla_norm, ev_w_out,
           od_w_in, od_conv_w, od_conv_b, od_w_a, od_b_a, od_w_i, od_b_i, od_lambda, od_w_out,
           ffn_w_up, ffn_conv_w, ffn_conv_b, ffn_w_down, final_norm):
    batch, seq, d = x_prompt.shape
    dec_batch, dec_seq, _ = x_sample.shape
    depth = w_ada.shape[0]
    n_even = ev_w_in.shape[0]
    geom = _Geom(batch, seq, dec_batch, dec_seq, min(ROW_TILE, dec_seq))
    n_seq = batch + dec_batch

    cond = jnp.concatenate([c_ctx[None, :], c], axis=0)
    cond = jnp.pad(cond, ((0, SUBLANES - cond.shape[0] % SUBLANES), (0, 0)))
    mod = _ada(cond, w_ada, b_ada)

    x = jnp.concatenate([x_prompt.reshape(batch * seq, d), x_sample.reshape(dec_batch * dec_seq, d)], axis=0)
    cos_t, sin_t = _rope_tables(dec_seq)
    gla_consts = _gla_constants(GLA_CHUNK)
    akv = A_KV_HEADS * HEAD_DIM
    bqk = B_HEADS * B_DK
    d_ff = ffn_w_down.shape[1]
    n_ff = d_ff // FF_CHUNK

    new_k, new_v, new_gla, new_rg = [], [], [], []
    for layer in range(depth):
        j = layer // 2
        g_mix = norm_mix[layer][None, :]
        if layer % 2 == 0:
            w_in = ev_w_in[j]
            main = w_in.shape[1] - 2 * GATE_RANK
            w_in_p = jnp.pad(w_in, ((0, 0), (0, LANES - 2 * GATE_RANK))).astype(BF16)
            wg = jnp.zeros((LANES, 2 * bqk), F32)
            wg = wg.at[0:GATE_RANK, 0:bqk].set(ev_w_gate_f[j])
            wg = wg.at[GATE_RANK:2 * GATE_RANK, bqk:2 * bqk].set(ev_w_gate_b[j]).astype(BF16)
            bg = jnp.concatenate([ev_b_gate_f[j], ev_b_gate_b[j]])[None, :]
            assert main % LANES == 0
            qa, ka, va, qkb, vb, gb, la = _even_in(x, mod, layer, g_mix, w_in_p, wg, bg, cos_t, sin_t, geom)
            ck = cache_attn_k[:, j].reshape(dec_batch, -1, akv)
            cv = cache_attn_v[:, j].reshape(dec_batch, -1, akv)
            o_ctx, o_lat = _attention(qa, ka, va, ev_sink[j], ck, cv, geom)
            oa = jnp.concatenate([o_ctx, o_lat], axis=0)
            s0_all = jnp.concatenate([jnp.zeros((batch,) + state_gla.shape[2:], F32), state_gla[:, j]], axis=0)
            o_gla, s_fin = _gla(qkb, vb, la, s0_all, gla_consts, geom)
            x = _even_out(x, mod, layer, oa, o_gla, gb, ev_gla_norm[j][None, :], ev_w_out[j].astype(BF16), geom)
            new_k.append(ka[:geom.n_ctx].reshape(batch, seq, A_KV_HEADS, HEAD_DIM))
            new_v.append(va[:geom.n_ctx].reshape(batch, seq, A_KV_HEADS, HEAD_DIM))
            new_gla.append(s_fin[:batch])
        else:
            w_in = od_w_in[j].astype(BF16)
            d_rnn = w_in.shape[1] // 2
            grp = 2 * LANES
            wa = jnp.stack([_block_diag_groups(od_w_a[j, dr], grp) for dr in range(2)])
            wi = jnp.stack([_block_diag_groups(od_w_i[j, dr], grp) for dr in range(2)])
            wgate = jnp.concatenate([wa, wi], axis=-1).astype(BF16)
            ba = od_b_a[j].reshape(2, d_rnn // grp, 1, grp)
            bi = od_b_i[j].reshape(2, d_rnn // grp, 1, grp)
            bgate = jnp.concatenate([ba, bi], axis=-1)
            gy, a, u = _odd_in(x, mod, layer, g_mix, w_in[:, :d_rnn], w_in[:, d_rnn:], od_conv_w[j],
                               od_conv_b[j][None, :], wgate, bgate, od_lambda[j], geom)
            h0_all = jnp.concatenate([jnp.zeros((batch, 2, d_rnn), F32), state_rglru[:, j]], axis=0)
            h = _rglru_scan(a, u, h0_all.reshape(n_seq, 2, 1, d_rnn), geom)
            x = _odd_out(x, mod, layer, gy, h, od_w_out[j].astype(BF16), geom)
            hc = h[:, :geom.n_ctx].reshape(2, batch, seq, d_rnn)
            new_rg.append(jnp.stack([hc[0, :, -1], hc[1, :, 0]], axis=1))
        wu = ffn_w_up[layer].astype(BF16).reshape(d, 2, n_ff, FF_CHUNK).transpose(1, 2, 0, 3)
        cw = ffn_conv_w[layer].reshape(FFN_CONV_W, 2, n_ff, FF_CHUNK).transpose(1, 2, 0, 3)
        cb = ffn_conv_b[layer].reshape(2, n_ff, 1, FF_CHUNK)
        wd = ffn_w_down[layer].astype(BF16).reshape(n_ff, FF_CHUNK, d)
        x = _ffn(x, mod, layer, norm_ffn[layer][None, :], wu, cw, cb, wd, final_norm[None, :], geom,
                 final=(layer == depth - 1))

    y_prompt = x[:geom.n_ctx].reshape(batch, seq, d)
    y_sample = x[geom.n_ctx:].reshape(dec_batch, dec_seq, d)
    return (y_prompt, y_sample, jnp.stack(new_k, axis=1), jnp.stack(new_v, axis=1),
            jnp.stack(new_gla, axis=1), jnp.stack(new_rg, axis=1))
```

```python
import functools

import jax
import jax.numpy as jnp
import numpy as np
from jax import lax
from jax.experimental import pallas as pl
from jax.experimental.pallas import tpu as pltpu

F32 = jnp.float32
BF16 = jnp.bfloat16

EPS = 1e-6
NEG_INF = -1e30
GRID_W = 64
A_HEADS = 8
A_KV_HEADS = 2
HEAD_DIM = 64
WINDOW = 128
ATT_BLOCK = 128
ROPE_BASE = 10000.0
B_HEADS = 4
B_DK = 64
B_DV = 128
GATE_RANK = 16
GATE_NORMALIZER = 16.0
RG_BLOCKS = 16
RG_C = 8.0
CONV_W = 4
FFN_CONV_W = 3

LANES = 128
SUBLANES = 8
HALO = SUBLANES
ROW_TILE = 512
CONV_TILE = 256
GLA_CHUNK = 128
SCAN_TILE = 256
SCAN_COLS = 512
FF_CHUNK = 256
VMEM_LIMIT = 56 * 1024 * 1024


def _cparams(n_axes):
    return pltpu.CompilerParams(dimension_semantics=("arbitrary",) * n_axes,
                                vmem_limit_bytes=VMEM_LIMIT)


def _resident(shape):
    nd = len(shape)
    return pl.BlockSpec(shape, lambda *_: (0,) * nd, pipeline_mode=pl.Buffered(1))


def _silu(x):
    return x * jax.nn.sigmoid(x)


def _rms(x, g):
    return x * lax.rsqrt(jnp.mean(x * x, axis=-1, keepdims=True) + EPS) * g


def _dot(a, b):
    return jnp.dot(a, b, preferred_element_type=F32)


def _dot_nt(a, b):
    return lax.dot_general(a, b, (((1,), (1,)), ((), ())), preferred_element_type=F32)


def _dot_tn(a, b):
    return lax.dot_general(a, b, (((0,), (0,)), ((), ())), preferred_element_type=F32)


def _ada_kernel(c_ref, w_ref, b_ref, o_ref):
    s = _silu(c_ref[...])
    o_ref[...] = _dot(s.astype(BF16), w_ref[...].astype(BF16)) + b_ref[...]


def _ada(cond, w_ada, b_ada):
    depth, d, d6 = w_ada.shape
    nb = 4
    cb = d6 // nb
    rows = cond.shape[0]
    out = pl.pallas_call(
        _ada_kernel,
        grid=(depth, nb),
        in_specs=[pl.BlockSpec((rows, d), lambda l, j: (0, 0)),
                  pl.BlockSpec((None, d, cb), lambda l, j: (l, 0, j)),
                  pl.BlockSpec((None, 1, cb), lambda l, j: (l, 0, j))],
        out_specs=pl.BlockSpec((None, rows, cb), lambda l, j: (l, 0, j)),
        out_shape=jax.ShapeDtypeStruct((depth, rows, d6), F32),
        compiler_params=_cparams(2),
        name="ada",
    )(cond, w_ada, b_ada.reshape(depth, 1, d6))
    return out.reshape(depth, rows, 6, d)


class _Geom:
    def __init__(self, n_ctx_seq, ctx_len, n_lat_seq, lat_len, tile):
        assert ctx_len % tile == 0 or tile % ctx_len == 0
        assert lat_len % tile == 0
        self.n_ctx_seq, self.ctx_len, self.n_lat_seq, self.lat_len = n_ctx_seq, ctx_len, n_lat_seq, lat_len
        self.tile = tile
        self.n_ctx = n_ctx_seq * ctx_len
        self.n_lat = n_lat_seq * lat_len
        self.n = self.n_ctx + self.n_lat
        assert self.n_ctx % tile == 0
        self.ctx_tiles = self.n_ctx // tile
        self.lat_tiles_per_seq = lat_len // tile
        self.tiles = self.n // tile

    def mod_row(self, i):
        return jnp.where(i < self.ctx_tiles, 0, 1 + (i - self.ctx_tiles) // self.lat_tiles_per_seq)

    def lat_tile(self, i):
        return jnp.where(i < self.ctx_tiles, 0, (i - self.ctx_tiles) % self.lat_tiles_per_seq)

    def seq_edges(self, i):
        assert self.ctx_len % self.tile == 0
        is_ctx = i < self.ctx_tiles
        seq_len = jnp.where(is_ctx, self.ctx_len, self.lat_len)
        start = (i * self.tile - jnp.where(is_ctx, 0, self.n_ctx)) % seq_len
        return start == 0, start + self.tile == seq_len

    def seq_pos(self, i, rows, offset=0):
        r = lax.broadcasted_iota(jnp.int32, (rows, 1), 0) + (i * self.tile + offset)
        is_ctx = i < self.ctx_tiles
        seq_len = jnp.where(is_ctx, self.ctx_len, self.lat_len)
        base = jnp.where(is_ctx, 0, self.n_ctx)
        return (r - base) % seq_len, seq_len


def _halo_specs(geom, d):
    t = geom.tile
    per = t // HALO
    last = geom.n // HALO - 1
    return [pl.BlockSpec((HALO, d), lambda i: (jnp.maximum(i * per - 1, 0), 0)),
            pl.BlockSpec((t, d), lambda i: (i, 0)),
            pl.BlockSpec((HALO, d), lambda i: (jnp.minimum((i + 1) * per, last), 0))]


def _mod_spec(geom, layer, d):
    return pl.BlockSpec((None, None, 6, d), lambda i: (layer, geom.mod_row(i), 0, 0))


def _row_spec(geom, width, col=0):
    return pl.BlockSpec((geom.tile, width), lambda i: (i, col))


def _shift_rows(x_ext, shift, rows):
    if shift == 0:
        return x_ext[HALO:HALO + rows]
    return pltpu.roll(x_ext, shift % x_ext.shape[0], 0)[HALO:HALO + rows]


def _rope_apply(t, cos, sin):
    lane = lax.broadcasted_iota(jnp.int32, t.shape, 1)
    swapped = jnp.where((lane & 16) == 0, pltpu.roll(t, LANES - 16, 1), pltpu.roll(t, 16, 1))
    return t * cos + swapped * sin


def _even_in_kernel(x_ref, mod_ref, g_ref, w_ref, wg_ref, bg_ref, cos_ref, sin_ref,
                    qa_ref, ka_ref, va_ref, qkb_ref, vb_ref, gb_ref, la_ref, *, geom):
    i = pl.program_id(0)
    h = _rms(x_ref[...], g_ref[...]) * (1.0 + mod_ref[1:2, :]) + mod_ref[0:1, :]
    p = _dot(h.astype(BF16), w_ref[...])
    aq = A_HEADS * HEAD_DIM
    akv = A_KV_HEADS * HEAD_DIM
    bqk = B_HEADS * B_DK
    bv = B_HEADS * B_DV
    o = 0
    q_a = p[:, o:o + aq] * (HEAD_DIM ** -0.5); o += aq
    k_a = p[:, o:o + akv]; o += akv
    va_ref[...] = p[:, o:o + akv]; o += akv
    qkb_ref[:, 0:bqk] = p[:, o:o + bqk] * (B_DK ** -0.5); o += bqk
    qkb_ref[:, bqk:2 * bqk] = p[:, o:o + bqk]; o += bqk
    vb_ref[...] = p[:, o:o + bv]; o += bv
    gb_ref[...] = p[:, o:o + bv]; o += bv
    r = p[:, o:o + LANES]
    z = _dot(r.astype(BF16), wg_ref[...]) + bg_ref[...]
    la_ref[...] = jax.nn.log_sigmoid(z) * (1.0 / GATE_NORMALIZER)

    is_lat = i >= geom.ctx_tiles
    cos, sin = cos_ref[...], sin_ref[...]
    for j in range(aq // LANES):
        blk = q_a[:, j * LANES:(j + 1) * LANES]
        qa_ref[:, j * LANES:(j + 1) * LANES] = jnp.where(is_lat, _rope_apply(blk, cos, sin), blk)
    for j in range(akv // LANES):
        blk = k_a[:, j * LANES:(j + 1) * LANES]
        ka_ref[:, j * LANES:(j + 1) * LANES] = jnp.where(is_lat, _rope_apply(blk, cos, sin), blk)


def _even_in(x, mod, layer, g, w_in_p, wg, bg, cos_t, sin_t, geom):
    n, d = x.shape
    t = geom.tile
    aq = A_HEADS * HEAD_DIM
    akv = A_KV_HEADS * HEAD_DIM
    bqk = B_HEADS * B_DK
    bv = B_HEADS * B_DV
    widths = (aq, akv, akv, 2 * bqk, bv, bv, 2 * bqk)
    return pl.pallas_call(
        functools.partial(_even_in_kernel, geom=geom),
        grid=(geom.tiles,),
        in_specs=[_row_spec(geom, d), _mod_spec(geom, layer, d), _resident(g.shape),
                  _resident(w_in_p.shape), _resident(wg.shape), _resident(bg.shape),
                  pl.BlockSpec((t, LANES), lambda i: (geom.lat_tile(i), 0)),
                  pl.BlockSpec((t, LANES), lambda i: (geom.lat_tile(i), 0))],
        out_specs=[_row_spec(geom, w) for w in widths],
        out_shape=[jax.ShapeDtypeStruct((n, w), F32) for w in widths],
        compiler_params=_cparams(1),
        name="even_in",
    )(x, mod, g, w_in_p, wg, bg, cos_t, sin_t)


def _attend(q, k, v, mask, sink_ref, o_ref):
    lane = lax.broadcasted_iota(jnp.int32, k.shape, 1)
    lo = lane < HEAD_DIM
    k_sw = pltpu.roll(k, HEAD_DIM, 1)
    v_sw = pltpu.roll(v, HEAD_DIM, 1)
    zero = jnp.zeros_like(k)
    kk = [[jnp.where(lo, k, zero).astype(BF16), jnp.where(lo, zero, k_sw).astype(BF16)],
          [jnp.where(lo, k_sw, zero).astype(BF16), jnp.where(lo, zero, k).astype(BF16)]]
    vv = [[jnp.where(lo, v, zero).astype(BF16), jnp.where(lo, zero, v_sw).astype(BF16)],
          [jnp.where(lo, v_sw, zero).astype(BF16), jnp.where(lo, zero, v).astype(BF16)]]
    group = A_HEADS // A_KV_HEADS
    for pair in range(A_HEADS // 2):
        qp = q[:, pair * LANES:(pair + 1) * LANES].astype(BF16)
        acc = None
        for half in range(2):
            head = 2 * pair + half
            kv = head // group
            s = _dot_nt(qp, kk[kv][half])
            if mask is not None:
                s = jnp.where(mask, s, NEG_INF)
            snk = sink_ref[head]
            m = jnp.maximum(jnp.max(s, axis=1, keepdims=True), snk)
            e = jnp.exp(s - m)
            l = jnp.sum(e, axis=1, keepdims=True) + jnp.exp(snk - m)
            o = _dot((e / l).astype(BF16), vv[kv][half])
            acc = o if acc is None else acc + o
        o_ref[:, pair * LANES:(pair + 1) * LANES] = acc


def _attn_ctx_kernel(sink_ref, q_ref, k_ref, v_ref, o_ref):
    _attend(q_ref[...], k_ref[...], v_ref[...], None, sink_ref, o_ref)


def _attn_lat_kernel(sink_ref, q_ref, kp_ref, kc_ref, kn_ref, vp_ref, vc_ref, vn_ref, kx_ref, vx_ref, o_ref,
                     *, n_blocks):
    i = pl.program_id(1)
    blk = ATT_BLOCK
    k = jnp.concatenate([kp_ref[...], kc_ref[...], kn_ref[...], kx_ref[...]], axis=0)
    v = jnp.concatenate([vp_ref[...], vc_ref[...], vn_ref[...], vx_ref[...]], axis=0)
    nk = k.shape[0]
    qi = lax.broadcasted_iota(jnp.int32, (blk, nk), 0)
    kj = lax.broadcasted_iota(jnp.int32, (blk, nk), 1)
    rel = kj - blk - qi
    in_win = (jnp.abs(rel) <= WINDOW) & (kj < 3 * blk)
    in_win = in_win & ((kj >= blk) | (i > 0)) & ((kj < 2 * blk) | (i < n_blocks - 1))
    mask = in_win | (kj >= 3 * blk)
    _attend(q_ref[...], k, v, mask, sink_ref, o_ref)


def _attention(qa, ka, va, sink, cache_k, cache_v, geom):
    aq = qa.shape[1]
    akv = ka.shape[1]
    smem = pl.BlockSpec(memory_space=pltpu.SMEM)
    cl = geom.ctx_len
    o_ctx = pl.pallas_call(
        _attn_ctx_kernel,
        grid=(geom.n_ctx_seq,),
        in_specs=[smem,
                  pl.BlockSpec((cl, aq), lambda b: (b, 0)),
                  pl.BlockSpec((cl, akv), lambda b: (b, 0)),
                  pl.BlockSpec((cl, akv), lambda b: (b, 0))],
        out_specs=pl.BlockSpec((cl, aq), lambda b: (b, 0)),
        out_shape=jax.ShapeDtypeStruct((geom.n_ctx, aq), F32),
        compiler_params=_cparams(1),
        name="attn_ctx",
    )(sink, qa, ka, va)

    blk = ATT_BLOCK
    nb = geom.lat_len // blk
    base = geom.n_ctx // blk
    past = cache_k.shape[1]

    def cur(b, i):
        return (base + b * nb + i, 0)

    def prv(b, i):
        return (base + b * nb + jnp.maximum(i - 1, 0), 0)

    def nxt(b, i):
        return (base + b * nb + jnp.minimum(i + 1, nb - 1), 0)

    kv_spec = lambda f: pl.BlockSpec((blk, akv), f)
    cache_spec = pl.BlockSpec((None, past, akv), lambda b, i: (b, 0, 0))
    o_lat = pl.pallas_call(
        functools.partial(_attn_lat_kernel, n_blocks=nb),
        grid=(geom.n_lat_seq, nb),
        in_specs=[smem, pl.BlockSpec((blk, aq), cur),
                  kv_spec(prv), kv_spec(cur), kv_spec(nxt),
                  kv_spec(prv), kv_spec(cur), kv_spec(nxt),
                  cache_spec, cache_spec],
        out_specs=pl.BlockSpec((blk, aq), lambda b, i: (b * nb + i, 0)),
        out_shape=jax.ShapeDtypeStruct((geom.n_lat, aq), F32),
        compiler_params=_cparams(2),
        name="attn_lat",
    )(sink, qa, ka, ka, ka, va, va, va, cache_k, cache_v)
    return o_ctx, o_lat


def _gla_constants(c):
    levels = int(np.log2(c))
    r = np.arange(c)[:, None]
    t = np.arange(c)[None, :]
    secs = [(t <= r), (t > r)]
    is_q = []
    lv = np.full((c, c), -1, np.int32)
    lv[np.arange(c), np.arange(c)] = levels
    for lvl in range(levels):
        s = c >> (lvl + 1)
        pos = r % (2 * s)
        m = r - pos + s - 1
        q_side = pos >= s
        w = np.where(q_side, (t > m) & (t <= r), (t > r) & (t <= m))
        secs.append(w)
        is_q.append(np.broadcast_to(q_side, (c, 1)).astype(np.float32))
        i_idx = np.arange(c)[:, None]
        j_idx = np.arange(c)[None, :]
        same_pair = (i_idx // (2 * s)) == (j_idx // (2 * s))
        i_q = (i_idx % (2 * s)) >= s
        j_k = (j_idx % (2 * s)) < s
        lv[same_pair & i_q & j_k] = lvl
    secs.append(np.ones((SUBLANES, c), bool))
    w_f = np.concatenate([np.asarray(x, np.float32) for x in secs], axis=0)
    isq_f = np.concatenate(is_q, axis=1)
    w_b = np.concatenate([np.asarray(x, np.float32)[::-1, ::-1] if x.shape[0] == c else np.asarray(x, np.float32)
                          for x in secs], axis=0)
    isq_b = isq_f[::-1]
    lv_b = lv[::-1, ::-1]
    w_all = jnp.asarray(np.stack([w_f, w_b]), BF16)
    isq = np.zeros((2, c, LANES), np.float32)
    isq[0, :, :levels] = isq_f
    isq[1, :, :levels] = isq_b
    return w_all, jnp.asarray(isq), jnp.asarray(np.stack([lv, lv_b]))


def _gla_kernel(qk_q_ref, qk_k_ref, v_ref, la_ref, s0_ref, w_ref, isq_ref, lv_ref,
                o_ref, sfin_ref, st_ref, *, first_last):
    d = pl.program_id(0)
    n = pl.program_id(2)
    c = GLA_CHUNK
    levels = int(np.log2(c))
    is_first, is_last = first_last(d, n)
    lane = lax.broadcasted_iota(jnp.int32, (c, LANES), 1)
    head_lo = lane < B_DK
    row2 = lax.broadcasted_iota(jnp.int32, (2 * B_DV, LANES), 0)
    lane2 = lax.broadcasted_iota(jnp.int32, (2 * B_DV, LANES), 1)
    diag_blocks = (row2 < B_DV) == (lane2 < B_DK)

    @pl.when(is_first)
    def _():
        s0 = jnp.concatenate([s0_ref[0], s0_ref[1]], axis=0)
        s0t = s0.T
        st_ref[...] = jnp.where(diag_blocks, jnp.concatenate([s0t, s0t], axis=0), 0.0)

    q = qk_q_ref[...]
    k = qk_k_ref[...]
    v = v_ref[...]
    la = la_ref[...]
    la_hi = la.astype(BF16)
    r1 = la - la_hi.astype(F32)
    la_mid = r1.astype(BF16)
    la_lo = (r1 - la_mid.astype(F32)).astype(BF16)
    w = w_ref[...]
    ex = (_dot(w, la_hi) + _dot(w, la_mid)) + _dot(w, la_lo)
    e_q = jnp.exp(ex[0:c])
    e_k = jnp.exp(ex[c:2 * c])
    a_chunk = jnp.exp(ex[(levels + 2) * c:(levels + 2) * c + 1])
    k_b = k.astype(BF16)
    v_b = v.astype(BF16)
    lv = lv_ref[...]
    isq = isq_ref[...]
    zero = jnp.zeros_like(q)

    q_heads = [jnp.where(head_lo, q, zero), jnp.where(head_lo, zero, q)]
    a = [jnp.where(lv == levels, _dot_nt(qh.astype(BF16), k_b), 0.0) for qh in q_heads]
    for lvl in range(levels):
        e = jnp.exp(ex[(2 + lvl) * c:(3 + lvl) * c])
        q_side = isq[:, lvl:lvl + 1] > 0.5
        kt = jnp.where(q_side, zero, k * e).astype(BF16)
        qe = jnp.where(q_side, q * e, zero)
        for hh in range(2):
            qt = jnp.where(head_lo if hh == 0 else ~head_lo, qe, zero).astype(BF16)
            a[hh] = jnp.where(lv == lvl, _dot_nt(qt, kt), a[hh])

    st = st_ref[...]
    o_inter = _dot_nt((q * e_q).astype(BF16), st.astype(BF16))
    for hh in range(2):
        o_intra = _dot(a[hh].astype(BF16), v_b[:, hh * B_DV:(hh + 1) * B_DV])
        o_ref[:, hh * B_DV:(hh + 1) * B_DV] = o_intra + o_inter[:, hh * B_DV:(hh + 1) * B_DV]
    upd = _dot_tn(v_b, (k * e_k).astype(BF16))
    st_new = jnp.where(diag_blocks, a_chunk * st + upd, 0.0)
    st_ref[...] = st_new

    @pl.when(is_last)
    def _():
        for hh in range(2):
            blk = st_new[hh * B_DV:(hh + 1) * B_DV, :].T
            sfin_ref[hh] = blk[hh * B_DK:(hh + 1) * B_DK, :]


def _gla(qkb, vb, la, s0_all, consts, geom):
    n = qkb.shape[0]
    c = GLA_CHUNK
    w_all, isq, lv = consts
    n_chunks = n // c
    ctx_chunks = geom.n_ctx // c
    cpc = geom.ctx_len // c
    cpl = geom.lat_len // c
    n_seq = geom.n_ctx_seq + geom.n_lat_seq
    pairs = B_HEADS // 2

    def chunk(d, nn):
        return jnp.where(d == 0, nn, n_chunks - 1 - nn)

    def seq_of(g):
        return jnp.where(g < ctx_chunks, g // cpc, geom.n_ctx_seq + (g - ctx_chunks) // cpl)

    def first_last(d, nn):
        g = chunk(d, nn)
        pos = jnp.where(g < ctx_chunks, g % cpc, (g - ctx_chunks) % cpl)
        per = jnp.where(g < ctx_chunks, cpc, cpl)
        at_start = pos == 0
        at_end = pos == per - 1
        return jnp.where(d == 0, at_start, at_end), jnp.where(d == 0, at_end, at_start)

    state_spec = pl.BlockSpec((None, None, 2, B_DK, B_DV), lambda d, p, nn: (seq_of(chunk(d, nn)), d, p, 0, 0))
    o, sfin = pl.pallas_call(
        functools.partial(_gla_kernel, first_last=first_last),
        grid=(2, pairs, n_chunks),
        in_specs=[pl.BlockSpec((c, LANES), lambda d, p, nn: (chunk(d, nn), p)),
                  pl.BlockSpec((c, LANES), lambda d, p, nn: (chunk(d, nn), pairs + p)),
                  pl.BlockSpec((c, 2 * B_DV), lambda d, p, nn: (chunk(d, nn), p)),
                  pl.BlockSpec((c, LANES), lambda d, p, nn: (chunk(d, nn), d * pairs + p)),
                  state_spec,
                  pl.BlockSpec((None,) + w_all.shape[1:], lambda d, p, nn: (d, 0, 0)),
                  pl.BlockSpec((None,) + isq.shape[1:], lambda d, p, nn: (d, 0, 0)),
                  pl.BlockSpec((None,) + lv.shape[1:], lambda d, p, nn: (d, 0, 0))],
        out_specs=[pl.BlockSpec((None, c, 2 * B_DV), lambda d, p, nn: (d, chunk(d, nn), p)),
                   state_spec],
        out_shape=[jax.ShapeDtypeStruct((2, n, B_HEADS * B_DV), F32),
                   jax.ShapeDtypeStruct((n_seq, 2, B_HEADS, B_DK, B_DV), F32)],
        scratch_shapes=[pltpu.VMEM((2 * B_DV, LANES), F32)],
        compiler_params=_cparams(3),
        name="gla",
    )(qkb, qkb, vb, la, s0_all, w_all, isq, lv)
    return o, sfin


def _even_out_kernel(x_ref, mod_ref, oa_ref, of_ref, ob_ref, gb_ref, gn_ref, w_ref, o_ref):
    ob = of_ref[...] + ob_ref[...]
    gb = gb_ref[...]
    gn = gn_ref[...]
    aq = oa_ref.shape[1]
    acc = _dot(oa_ref[...].astype(BF16), w_ref[0:aq, :])
    for hh in range(B_HEADS):
        blk = _rms(ob[:, hh * B_DV:(hh + 1) * B_DV], gn) * _silu(gb[:, hh * B_DV:(hh + 1) * B_DV])
        acc = acc + _dot(blk.astype(BF16), w_ref[aq + hh * B_DV:aq + (hh + 1) * B_DV, :])
    o_ref[...] = x_ref[...] + mod_ref[2:3, :] * acc


def _even_out(x, mod, layer, oa, o_gla, gb, gn, w_out, geom):
    n, d = x.shape
    t = geom.tile
    bv = gb.shape[1]
    return pl.pallas_call(
        _even_out_kernel,
        grid=(geom.tiles,),
        in_specs=[_row_spec(geom, d), _mod_spec(geom, layer, d), _row_spec(geom, oa.shape[1]),
                  pl.BlockSpec((None, t, bv), lambda i: (0, i, 0)),
                  pl.BlockSpec((None, t, bv), lambda i: (1, i, 0)),
                  _row_spec(geom, bv), _resident(gn.shape), _resident(w_out.shape)],
        out_specs=_row_spec(geom, d),
        out_shape=jax.ShapeDtypeStruct((n, d), F32),
        compiler_params=_cparams(1),
        name="even_out",
    )(x, mod, oa, o_gla, o_gla, gb, gn, w_out)


def _odd_in_kernel(xp_ref, x_ref, xn_ref, mod_ref, g_ref, w_ref, cw_ref, cb_ref, gy_ref, xr_ref, s_ref, *, geom):
    i = pl.program_id(0)
    t = geom.tile
    first, last = geom.seq_edges(i)
    h_ext = _modulated_ext(xp_ref, x_ref, xn_ref, g_ref, mod_ref[0:1, :], mod_ref[1:2, :], first, last)
    yx_ext = _dot(h_ext, w_ref[...])
    d_rnn = gy_ref.shape[1]
    gy_ref[...] = jax.nn.gelu(yx_ext[HALO:HALO + t, 0:d_rnn])
    left = CONV_W // 2
    for sl in range(d_rnn // LANES):
        cols = slice(sl * LANES, (sl + 1) * LANES)
        s_ref[sl] = yx_ext[:, d_rnn + sl * LANES:d_rnn + (sl + 1) * LANES]
        xr = cb_ref[:, cols]
        for tap in range(CONV_W):
            xr = xr + s_ref[sl, pl.ds(HALO - left + tap, t), :] * cw_ref[tap:tap + 1, cols]
        xr_ref[:, cols] = xr


def _odd_in(x, mod, layer, g, w_in, cw, cb, geom):
    n, d = x.shape
    t = geom.tile
    d_rnn = w_in.shape[1] // 2
    return pl.pallas_call(
        functools.partial(_odd_in_kernel, geom=geom),
        grid=(geom.tiles,),
        in_specs=_halo_specs(geom, d) + [_mod_spec(geom, layer, d)] +
                 [_resident(a.shape) for a in (g, w_in, cw, cb)],
        out_specs=[_row_spec(geom, d_rnn), _row_spec(geom, d_rnn)],
        out_shape=[jax.ShapeDtypeStruct((n, d_rnn), F32), jax.ShapeDtypeStruct((n, d_rnn), F32)],
        scratch_shapes=[pltpu.VMEM((d_rnn // LANES, t + 2 * HALO, LANES), F32)],
        compiler_params=_cparams(1),
        name="odd_in",
    )(x, x, x, mod, g, w_in, cw, cb)


def _scan_slab(a, u, carry, reverse):
    rows = a.shape[0]
    groups = rows // SUBLANES
    a3 = a.reshape(groups, SUBLANES, LANES)
    u3 = u.reshape(groups, SUBLANES, LANES)
    sub = lax.broadcasted_iota(jnp.int32, a3.shape, 1)
    k = 1
    while k < SUBLANES:
        shift = SUBLANES - k if reverse else k
        ok = sub < SUBLANES - k if reverse else sub >= k
        a_s = jnp.where(ok, pltpu.roll(a3, shift, 1), 1.0)
        u_s = jnp.where(ok, pltpu.roll(u3, shift, 1), 0.0)
        u3 = a3 * u_s + u3
        a3 = a3 * a_s
        k *= 2
    hs = [None] * groups
    edge = 0 if reverse else SUBLANES - 1
    for gi in (reversed(range(groups)) if reverse else range(groups)):
        h = a3[gi] * carry + u3[gi]
        hs[gi] = h
        carry = h[edge:edge + 1, :]
    return jnp.concatenate(hs, axis=0), carry


def _odd_scan_kernel(xr_ref, wg_ref, bg_ref, lam_ref, h0_ref, h_ref, carry_ref, *, first_of, reverse):
    n = pl.program_id(0)

    @pl.when(first_of(n))
    def _():
        carry_ref[...] = h0_ref[...]

    xr = xr_ref[...]
    xr_b = xr.astype(BF16)
    log_sig_lam = jax.nn.log_sigmoid(lam_ref[...])
    n_grp, grp = wg_ref.shape[0], wg_ref.shape[1]
    for gi in range(n_grp):
        cols = slice(gi * grp, (gi + 1) * grp)
        ri = _dot(xr_b[:, cols], wg_ref[gi]) + bg_ref[gi]
        r_gate = jax.nn.sigmoid(ri[:, :grp])
        i_gate = jax.nn.sigmoid(ri[:, grp:])
        log_a = RG_C * r_gate * log_sig_lam[:, cols]
        a = jnp.exp(log_a)
        u = jnp.sqrt(-jnp.tanh(log_a) * (a * a + 1.0)) * (i_gate * xr[:, cols])
        for sl in range(grp // LANES):
            lanes = slice(gi * grp + sl * LANES, gi * grp + (sl + 1) * LANES)
            h, carry = _scan_slab(a[:, sl * LANES:(sl + 1) * LANES], u[:, sl * LANES:(sl + 1) * LANES],
                                  carry_ref[:, lanes], reverse)
            h_ref[:, lanes] = h
            carry_ref[:, lanes] = carry


def _odd_scan(xr, wg, bg, lam, h0_all, geom, direction):
    n, d = xr.shape
    ts = SCAN_TILE
    n_tiles = n // ts
    ctx_tiles = geom.n_ctx // ts
    tpc = geom.ctx_len // ts
    tpl = geom.lat_len // ts
    reverse = direction == 1

    def tile(nn):
        return n_tiles - 1 - nn if reverse else nn

    def seq_of(g):
        return jnp.where(g < ctx_tiles, g // tpc, geom.n_ctx_seq + (g - ctx_tiles) // tpl)

    def first_of(nn):
        g = tile(nn)
        pos = jnp.where(g < ctx_tiles, g % tpc, (g - ctx_tiles) % tpl)
        per = jnp.where(g < ctx_tiles, tpc, tpl)
        return pos == per - 1 if reverse else pos == 0

    blk = pl.BlockSpec((ts, d), lambda nn: (tile(nn), 0))
    return pl.pallas_call(
        functools.partial(_odd_scan_kernel, first_of=first_of, reverse=reverse),
        grid=(n_tiles,),
        in_specs=[blk, _resident(wg.shape), _resident(bg.shape), _resident(lam.shape),
                  pl.BlockSpec((None, None, 1, d), lambda nn: (seq_of(tile(nn)), direction, 0, 0))],
        out_specs=blk,
        out_shape=jax.ShapeDtypeStruct((n, d), F32),
        scratch_shapes=[pltpu.VMEM((1, d), F32)],
        compiler_params=_cparams(1),
        name="odd_scan",
    )(xr, wg, bg, lam, h0_all)


def _odd_out_kernel(x_ref, mod_ref, gy_ref, hf_ref, hb_ref, w_ref, o_ref):
    z = gy_ref[...] * (hf_ref[...] + hb_ref[...])
    o_ref[...] = x_ref[...] + mod_ref[2:3, :] * _dot(z.astype(BF16), w_ref[...])


def _odd_out(x, mod, layer, gy, h_f, h_b, w_out, geom):
    n, d = x.shape
    dr = gy.shape[1]
    return pl.pallas_call(
        _odd_out_kernel,
        grid=(geom.tiles,),
        in_specs=[_row_spec(geom, d), _mod_spec(geom, layer, d), _row_spec(geom, dr),
                  _row_spec(geom, dr), _row_spec(geom, dr), _resident(w_out.shape)],
        out_specs=_row_spec(geom, d),
        out_shape=jax.ShapeDtypeStruct((n, d), F32),
        compiler_params=_cparams(1),
        name="odd_out",
    )(x, mod, gy, h_f, h_b, w_out)


def _modulated_ext(xp_ref, x_ref, xn_ref, g_ref, shift, scale, first, last):
    g = g_ref[...]
    h_p = _rms(xp_ref[...], g) * (1.0 + scale) + shift
    h_c = _rms(x_ref[...], g) * (1.0 + scale) + shift
    h_n = _rms(xn_ref[...], g) * (1.0 + scale) + shift
    h_p = jnp.where(first, 0.0, h_p)
    h_n = jnp.where(last, 0.0, h_n)
    return jnp.concatenate([h_p, h_c, h_n], axis=0).astype(BF16)


def _ffn_kernel(xp_ref, x_ref, xn_ref, mod_ref, g_ref, wu_ref, cw_ref, cb_ref, wd_ref, fn_ref, o_ref, s_ref,
                *, geom, final):
    i = pl.program_id(0)
    t = geom.tile
    first, last = geom.seq_edges(i)
    h_ext = _modulated_ext(xp_ref, x_ref, xn_ref, g_ref, mod_ref[3:4, :], mod_ref[4:5, :], first, last)
    n_chunks = wd_ref.shape[0]
    n_slab = wu_ref.shape[3] // LANES
    def project(c):
        for part in range(2):
            u_ext = _dot(h_ext, wu_ref[part, c])
            for sl in range(n_slab):
                s_ref[c % 2, part * n_slab + sl] = u_ext[:, sl * LANES:(sl + 1) * LANES]

    acc = None
    project(0)
    for c in range(n_chunks):
        if c + 1 < n_chunks:
            project(c + 1)
        acts = []
        for sl in range(n_slab):
            cols = slice(sl * LANES, (sl + 1) * LANES)
            halves = []
            for part in range(2):
                slab = s_ref.at[c % 2, part * n_slab + sl]
                cw = cw_ref[part, c]
                halves.append(slab[pl.ds(HALO - 1, t), :] * cw[0:1, cols]
                              + slab[pl.ds(HALO, t), :] * cw[1:2, cols]
                              + slab[pl.ds(HALO + 1, t), :] * cw[2:3, cols]
                              + cb_ref[part, c][:, cols])
            acts.append((_silu(halves[0]) * halves[1]).astype(BF16))
        part_out = _dot(jnp.concatenate(acts, axis=1), wd_ref[c])
        acc = part_out if acc is None else acc + part_out
    y = x_ref[...] + mod_ref[5:6, :] * acc
    if final:
        y = _rms(y, fn_ref[...])
    o_ref[...] = y


def _ffn(x, mod, layer, g, wu, cw, cb, wd, fn, geom, final):
    n, d = x.shape
    n_slab = wu.shape[3] // LANES
    return pl.pallas_call(
        functools.partial(_ffn_kernel, geom=geom, final=final),
        grid=(geom.tiles,),
        in_specs=_halo_specs(geom, d) + [_mod_spec(geom, layer, d)] +
                 [_resident(a.shape) for a in (g, wu, cw, cb, wd, fn)],
        out_specs=_row_spec(geom, d),
        out_shape=jax.ShapeDtypeStruct((n, d), F32),
        scratch_shapes=[pltpu.VMEM((2, 2 * n_slab, geom.tile + 2 * HALO, LANES), F32)],
        compiler_params=_cparams(1),
        name="ffn",
    )(x, x, x, mod, g, wu, cw, cb, wd, fn)


def _rope_tables(t_len):
    rows = t_len // GRID_W
    row = jnp.repeat(jnp.arange(rows, dtype=F32), GRID_W)
    col = jnp.tile(jnp.arange(GRID_W, dtype=F32), rows)
    n_freq = HEAD_DIM // 4
    inv_freq = jnp.power(ROPE_BASE, -jnp.arange(n_freq, dtype=F32) / n_freq)
    ang_r = row[:, None] * inv_freq
    ang_c = col[:, None] * inv_freq
    cos_h = jnp.concatenate([jnp.cos(ang_r)] * 2 + [jnp.cos(ang_c)] * 2, axis=1)
    sin_h = jnp.concatenate([-jnp.sin(ang_r), jnp.sin(ang_r), -jnp.sin(ang_c), jnp.sin(ang_c)], axis=1)
    reps = LANES // HEAD_DIM
    return jnp.tile(cos_h, (1, reps)), jnp.tile(sin_h, (1, reps))


def _block_diag_groups(w, grp):
    nb, bw, _ = w.shape
    per = grp // bw
    w = w.reshape(nb // per, per, bw, bw)
    eye = jnp.eye(per, dtype=w.dtype)
    return jnp.einsum('gpij,pq->gpiqj', w, eye).reshape(nb // per, grp, grp)


def kernel(x_prompt, x_sample, cache_attn_k, cache_attn_v, state_gla, state_rglru, c, c_ctx,
           norm_mix, norm_ffn, w_ada, b_ada,
           ev_w_in, ev_sink, ev_w_gate_f, ev_b_gate_f, ev_w_gate_b, ev_b_gate_b, ev_gla_norm, ev_w_out,
           od_w_in, od_conv_w, od_conv_b, od_w_a, od_b_a, od_w_i, od_b_i, od_lambda, od_w_out,
           ffn_w_up, ffn_conv_w, ffn_conv_b, ffn_w_down, final_norm):
    batch, seq, d = x_prompt.shape
    dec_batch, dec_seq, _ = x_sample.shape
    depth = w_ada.shape[0]
    n_even = ev_w_in.shape[0]
    geom = _Geom(batch, seq, dec_batch, dec_seq, min(ROW_TILE, dec_seq))
    geom_c = _Geom(batch, seq, dec_batch, dec_seq, min(CONV_TILE, seq))
    n_seq = batch + dec_batch

    cond = jnp.concatenate([c_ctx[None, :], c], axis=0)
    cond = jnp.pad(cond, ((0, SUBLANES - cond.shape[0] % SUBLANES), (0, 0)))
    mod = _ada(cond, w_ada, b_ada)

    x = jnp.concatenate([x_prompt.reshape(batch * seq, d), x_sample.reshape(dec_batch * dec_seq, d)], axis=0)
    cos_t, sin_t = _rope_tables(dec_seq)
    gla_consts = _gla_constants(GLA_CHUNK)
    akv = A_KV_HEADS * HEAD_DIM
    bqk = B_HEADS * B_DK
    d_ff = ffn_w_down.shape[1]
    n_ff = d_ff // FF_CHUNK

    new_k, new_v, new_gla, new_rg = [], [], [], []
    for layer in range(depth):
        j = layer // 2
        g_mix = norm_mix[layer][None, :]
        if layer % 2 == 0:
            w_in = ev_w_in[j]
            main = w_in.shape[1] - 2 * GATE_RANK
            w_in_p = jnp.pad(w_in, ((0, 0), (0, LANES - 2 * GATE_RANK))).astype(BF16)
            wg = jnp.zeros((LANES, 2 * bqk), F32)
            wg = wg.at[0:GATE_RANK, 0:bqk].set(ev_w_gate_f[j])
            wg = wg.at[GATE_RANK:2 * GATE_RANK, bqk:2 * bqk].set(ev_w_gate_b[j]).astype(BF16)
            bg = jnp.concatenate([ev_b_gate_f[j], ev_b_gate_b[j]])[None, :]
            assert main % LANES == 0
            qa, ka, va, qkb, vb, gb, la = _even_in(x, mod, layer, g_mix, w_in_p, wg, bg, cos_t, sin_t, geom)
            ck = cache_attn_k[:, j].reshape(dec_batch, -1, akv)
            cv = cache_attn_v[:, j].reshape(dec_batch, -1, akv)
            o_ctx, o_lat = _attention(qa, ka, va, ev_sink[j], ck, cv, geom)
            oa = jnp.concatenate([o_ctx, o_lat], axis=0)
            s0_all = jnp.concatenate([jnp.zeros((batch,) + state_gla.shape[2:], F32), state_gla[:, j]], axis=0)
            o_gla, s_fin = _gla(qkb, vb, la, s0_all, gla_consts, geom)
            x = _even_out(x, mod, layer, oa, o_gla, gb, ev_gla_norm[j][None, :], ev_w_out[j].astype(BF16), geom)
            new_k.append(ka[:geom.n_ctx].reshape(batch, seq, A_KV_HEADS, HEAD_DIM))
            new_v.append(va[:geom.n_ctx].reshape(batch, seq, A_KV_HEADS, HEAD_DIM))
            new_gla.append(s_fin[:batch])
        else:
            w_in = od_w_in[j].astype(BF16)
            d_rnn = w_in.shape[1] // 2
            grp = 2 * LANES
            wa = jnp.stack([_block_diag_groups(od_w_a[j, dr], grp) for dr in range(2)])
            wi = jnp.stack([_block_diag_groups(od_w_i[j, dr], grp) for dr in range(2)])
            wgate = jnp.concatenate([wa, wi], axis=-1).astype(BF16)
            ba = od_b_a[j].reshape(2, d_rnn // grp, 1, grp)
            bi = od_b_i[j].reshape(2, d_rnn // grp, 1, grp)
            bgate = jnp.concatenate([ba, bi], axis=-1)
            gy, xr = _odd_in(x, mod, layer, g_mix, w_in, od_conv_w[j], od_conv_b[j][None, :], geom_c)
            h0_all = jnp.concatenate([jnp.zeros((batch, 2, d_rnn), F32), state_rglru[:, j]], axis=0)
            h0_all = h0_all.reshape(n_seq, 2, 1, d_rnn)
            h_f, h_b = [_odd_scan(xr, wgate[dr], bgate[dr], od_lambda[j, dr][None, :], h0_all, geom, dr)
                        for dr in range(2)]
            x = _odd_out(x, mod, layer, gy, h_f, h_b, od_w_out[j].astype(BF16), geom)
            new_rg.append(jnp.stack([h_f[:geom.n_ctx].reshape(batch, seq, d_rnn)[:, -1],
                                     h_b[:geom.n_ctx].reshape(batch, seq, d_rnn)[:, 0]], axis=1))
        wu = ffn_w_up[layer].astype(BF16).reshape(d, 2, n_ff, FF_CHUNK).transpose(1, 2, 0, 3)
        cw = ffn_conv_w[layer].reshape(FFN_CONV_W, 2, n_ff, FF_CHUNK).transpose(1, 2, 0, 3)
        cb = ffn_conv_b[layer].reshape(2, n_ff, 1, FF_CHUNK)
        wd = ffn_w_down[layer].astype(BF16).reshape(n_ff, FF_CHUNK, d)
        x = _ffn(x, mod, layer, norm_ffn[layer][None, :], wu, cw, cb, wd, final_norm[None, :], geom_c,
                 final=(layer == depth - 1))

    y_prompt = x[:geom.n_ctx].reshape(batch, seq, d)
    y_sample = x[geom.n_ctx:].reshape(dec_batch, dec_seq, d)
    return (y_prompt, y_sample, jnp.stack(new_k, axis=1), jnp.stack(new_v, axis=1),
            jnp.stack(new_gla, axis=1), jnp.stack(new_rg, axis=1))
```

```python
import functools

import jax
import jax.numpy as jnp
import numpy as np
from jax import lax
from jax.experimental import pallas as pl
from jax.experimental.pallas import tpu as pltpu

F32 = jnp.float32
BF16 = jnp.bfloat16

EPS = 1e-6
NEG_INF = -1e30
GRID_W = 64
A_HEADS = 8
A_KV_HEADS = 2
HEAD_DIM = 64
WINDOW = 128
ATT_BLOCK = 128
ROPE_BASE = 10000.0
B_HEADS = 4
B_DK = 64
B_DV = 128
GATE_RANK = 16
GATE_NORMALIZER = 16.0
RG_BLOCKS = 16
RG_C = 8.0
CONV_W = 4
FFN_CONV_W = 3

LANES = 128
SUBLANES = 8
HALO = SUBLANES
ROW_TILE = 512
CONV_TILE = 256
GLA_CHUNK = 128
GLA_STEP = 256
SCAN_TILE = 256
SCAN_COLS = 512
FF_CHUNK = 256
VMEM_LIMIT = 56 * 1024 * 1024


def _cparams(n_axes):
    return pltpu.CompilerParams(dimension_semantics=("arbitrary",) * n_axes,
                                vmem_limit_bytes=VMEM_LIMIT)


def _resident(shape):
    nd = len(shape)
    return pl.BlockSpec(shape, lambda *_: (0,) * nd, pipeline_mode=pl.Buffered(1))


def _silu(x):
    return x * jax.nn.sigmoid(x)


def _rms(x, g):
    return x * lax.rsqrt(jnp.mean(x * x, axis=-1, keepdims=True) + EPS) * g


def _dot(a, b):
    return jnp.dot(a, b, preferred_element_type=F32)


def _dot_nt(a, b):
    return lax.dot_general(a, b, (((1,), (1,)), ((), ())), preferred_element_type=F32)


def _dot_tn(a, b):
    return lax.dot_general(a, b, (((0,), (0,)), ((), ())), preferred_element_type=F32)


def _ada_kernel(c_ref, w_ref, b_ref, o_ref):
    s = _silu(c_ref[...])
    o_ref[...] = _dot(s.astype(BF16), w_ref[...].astype(BF16)) + b_ref[...]


def _ada(cond, w_ada, b_ada):
    depth, d, d6 = w_ada.shape
    nb = 4
    cb = d6 // nb
    rows = cond.shape[0]
    out = pl.pallas_call(
        _ada_kernel,
        grid=(depth, nb),
        in_specs=[pl.BlockSpec((rows, d), lambda l, j: (0, 0)),
                  pl.BlockSpec((None, d, cb), lambda l, j: (l, 0, j)),
                  pl.BlockSpec((None, 1, cb), lambda l, j: (l, 0, j))],
        out_specs=pl.BlockSpec((None, rows, cb), lambda l, j: (l, 0, j)),
        out_shape=jax.ShapeDtypeStruct((depth, rows, d6), F32),
        compiler_params=_cparams(2),
        name="ada",
    )(cond, w_ada, b_ada.reshape(depth, 1, d6))
    return out.reshape(depth, rows, 6, d)


class _Geom:
    def __init__(self, n_ctx_seq, ctx_len, n_lat_seq, lat_len, tile):
        assert ctx_len % tile == 0 or tile % ctx_len == 0
        assert lat_len % tile == 0
        self.n_ctx_seq, self.ctx_len, self.n_lat_seq, self.lat_len = n_ctx_seq, ctx_len, n_lat_seq, lat_len
        self.tile = tile
        self.n_ctx = n_ctx_seq * ctx_len
        self.n_lat = n_lat_seq * lat_len
        self.n = self.n_ctx + self.n_lat
        assert self.n_ctx % tile == 0
        self.ctx_tiles = self.n_ctx // tile
        self.lat_tiles_per_seq = lat_len // tile
        self.tiles = self.n // tile

    def mod_row(self, i):
        return jnp.where(i < self.ctx_tiles, 0, 1 + (i - self.ctx_tiles) // self.lat_tiles_per_seq)

    def lat_tile(self, i):
        return jnp.where(i < self.ctx_tiles, 0, (i - self.ctx_tiles) % self.lat_tiles_per_seq)

    def seq_edges(self, i):
        assert self.ctx_len % self.tile == 0
        is_ctx = i < self.ctx_tiles
        seq_len = jnp.where(is_ctx, self.ctx_len, self.lat_len)
        start = (i * self.tile - jnp.where(is_ctx, 0, self.n_ctx)) % seq_len
        return start == 0, start + self.tile == seq_len

    def seq_pos(self, i, rows, offset=0):
        r = lax.broadcasted_iota(jnp.int32, (rows, 1), 0) + (i * self.tile + offset)
        is_ctx = i < self.ctx_tiles
        seq_len = jnp.where(is_ctx, self.ctx_len, self.lat_len)
        base = jnp.where(is_ctx, 0, self.n_ctx)
        return (r - base) % seq_len, seq_len


def _halo_specs(geom, d):
    t = geom.tile
    per = t // HALO
    last = geom.n // HALO - 1
    return [pl.BlockSpec((HALO, d), lambda i: (jnp.maximum(i * per - 1, 0), 0)),
            pl.BlockSpec((t, d), lambda i: (i, 0)),
            pl.BlockSpec((HALO, d), lambda i: (jnp.minimum((i + 1) * per, last), 0))]


def _mod_spec(geom, layer, d):
    return pl.BlockSpec((None, None, 6, d), lambda i: (layer, geom.mod_row(i), 0, 0))


def _row_spec(geom, width, col=0):
    return pl.BlockSpec((geom.tile, width), lambda i: (i, col))


def _shift_rows(x_ext, shift, rows):
    if shift == 0:
        return x_ext[HALO:HALO + rows]
    return pltpu.roll(x_ext, shift % x_ext.shape[0], 0)[HALO:HALO + rows]


def _rope_apply(t, cos, sin):
    lane = lax.broadcasted_iota(jnp.int32, t.shape, 1)
    swapped = jnp.where((lane & 16) == 0, pltpu.roll(t, LANES - 16, 1), pltpu.roll(t, 16, 1))
    return t * cos + swapped * sin


def _even_in_kernel(x_ref, mod_ref, g_ref, w_ref, wg_ref, bg_ref, cos_ref, sin_ref,
                    qa_ref, ka_ref, va_ref, gb_ref, gin_ref, *, geom):
    i = pl.program_id(0)
    h = _rms(x_ref[...], g_ref[...]) * (1.0 + mod_ref[1:2, :]) + mod_ref[0:1, :]
    p = _dot(h.astype(BF16), w_ref[...])
    aq = A_HEADS * HEAD_DIM
    akv = A_KV_HEADS * HEAD_DIM
    bqk = B_HEADS * B_DK
    bv = B_HEADS * B_DV
    o = 0
    q_a = p[:, o:o + aq] * (HEAD_DIM ** -0.5); o += aq
    k_a = p[:, o:o + akv]; o += akv
    va_ref[...] = p[:, o:o + akv]; o += akv
    gin_ref[:, 0:bqk] = p[:, o:o + bqk] * (B_DK ** -0.5); o += bqk
    gin_ref[:, bqk:2 * bqk] = p[:, o:o + bqk]; o += bqk
    gin_ref[:, 2 * bqk:2 * bqk + bv] = p[:, o:o + bv]; o += bv
    gb_ref[...] = p[:, o:o + bv]; o += bv
    r = p[:, o:o + LANES]
    z = _dot(r.astype(BF16), wg_ref[...]) + bg_ref[...]
    gin_ref[:, 2 * bqk + bv:4 * bqk + bv] = jax.nn.log_sigmoid(z) * (1.0 / GATE_NORMALIZER)

    is_lat = i >= geom.ctx_tiles
    cos, sin = cos_ref[...], sin_ref[...]
    for j in range(aq // LANES):
        blk = q_a[:, j * LANES:(j + 1) * LANES]
        qa_ref[:, j * LANES:(j + 1) * LANES] = jnp.where(is_lat, _rope_apply(blk, cos, sin), blk)
    for j in range(akv // LANES):
        blk = k_a[:, j * LANES:(j + 1) * LANES]
        ka_ref[:, j * LANES:(j + 1) * LANES] = jnp.where(is_lat, _rope_apply(blk, cos, sin), blk)


def _even_in(x, mod, layer, g, w_in_p, wg, bg, cos_t, sin_t, geom):
    n, d = x.shape
    t = geom.tile
    aq = A_HEADS * HEAD_DIM
    akv = A_KV_HEADS * HEAD_DIM
    bqk = B_HEADS * B_DK
    bv = B_HEADS * B_DV
    widths = (aq, akv, akv, bv, 4 * bqk + bv)
    return pl.pallas_call(
        functools.partial(_even_in_kernel, geom=geom),
        grid=(geom.tiles,),
        in_specs=[_row_spec(geom, d), _mod_spec(geom, layer, d), _resident(g.shape),
                  _resident(w_in_p.shape), _resident(wg.shape), _resident(bg.shape),
                  pl.BlockSpec((t, LANES), lambda i: (geom.lat_tile(i), 0)),
                  pl.BlockSpec((t, LANES), lambda i: (geom.lat_tile(i), 0))],
        out_specs=[_row_spec(geom, w) for w in widths],
        out_shape=[jax.ShapeDtypeStruct((n, w), F32) for w in widths],
        compiler_params=_cparams(1),
        name="even_in",
    )(x, mod, g, w_in_p, wg, bg, cos_t, sin_t)


def _attend(q, k, v, mask_fn, sink_ref, o_ref):
    nq, nk = q.shape[0], k.shape[0]
    lane = lax.broadcasted_iota(jnp.int32, k.shape, 1)
    lo = lane < HEAD_DIM
    k_sw = pltpu.roll(k, HEAD_DIM, 1)
    v_sw = pltpu.roll(v, HEAD_DIM, 1)
    kk = [[jnp.where(lo, k, 0.0).astype(BF16), jnp.where(lo, 0.0, k_sw).astype(BF16)],
          [jnp.where(lo, k_sw, 0.0).astype(BF16), jnp.where(lo, 0.0, k).astype(BF16)]]
    vv = [[jnp.where(lo, v, 0.0).astype(BF16), jnp.where(lo, 0.0, v_sw).astype(BF16)],
          [jnp.where(lo, v_sw, 0.0).astype(BF16), jnp.where(lo, 0.0, v).astype(BF16)]]
    group = A_HEADS // A_KV_HEADS
    mask = None
    if mask_fn is not None:
        mask = mask_fn(lax.broadcasted_iota(jnp.int32, (nq, nk), 0), lax.broadcasted_iota(jnp.int32, (nq, nk), 1))
    heads = range(A_HEADS)
    qb = [q[:, pair * LANES:(pair + 1) * LANES].astype(BF16) for pair in range(A_HEADS // 2)]
    s = [_dot_nt(qb[h // 2], kk[h // group][h % 2]) for h in heads]
    if mask is not None:
        s = [jnp.where(mask, sh, NEG_INF) for sh in s]
    m = [jnp.maximum(jnp.max(s[h], axis=1, keepdims=True), sink_ref[h]) for h in heads]
    e = [jnp.exp(s[h] - m[h]) for h in heads]
    inv = [1.0 / (jnp.sum(e[h], axis=1, keepdims=True) + jnp.exp(sink_ref[h] - m[h])) for h in heads]
    o = [_dot((e[h] * inv[h]).astype(BF16), vv[h // group][h % 2]) for h in heads]
    for pair in range(A_HEADS // 2):
        o_ref[:, pair * LANES:(pair + 1) * LANES] = o[2 * pair] + o[2 * pair + 1]


def _attn_ctx_kernel(sink_ref, q_ref, k_ref, v_ref, o_ref):
    _attend(q_ref[...], k_ref[...], v_ref[...], None, sink_ref, o_ref)


def _attn_lat_kernel(sink_ref, q_ref, kp_ref, kc_ref, kn_ref, vp_ref, vc_ref, vn_ref, kx_ref, vx_ref, o_ref,
                     *, n_blocks):
    i = pl.program_id(1)
    blk = ATT_BLOCK
    k = jnp.concatenate([kp_ref[...], kc_ref[...], kn_ref[...], kx_ref[...]], axis=0)
    v = jnp.concatenate([vp_ref[...], vc_ref[...], vn_ref[...], vx_ref[...]], axis=0)

    def mask_fn(qi, kj):
        in_win = (jnp.abs(kj - blk - qi) <= WINDOW) & (kj < 3 * blk)
        in_win = in_win & ((kj >= blk) | (i > 0)) & ((kj < 2 * blk) | (i < n_blocks - 1))
        return in_win | (kj >= 3 * blk)

    _attend(q_ref[...], k, v, mask_fn, sink_ref, o_ref)


def _attention(qa, ka, va, sink, cache_k, cache_v, geom):
    aq = qa.shape[1]
    akv = ka.shape[1]
    smem = pl.BlockSpec(memory_space=pltpu.SMEM)
    cl = geom.ctx_len
    o_ctx = pl.pallas_call(
        _attn_ctx_kernel,
        grid=(geom.n_ctx_seq,),
        in_specs=[smem,
                  pl.BlockSpec((cl, aq), lambda b: (b, 0)),
                  pl.BlockSpec((cl, akv), lambda b: (b, 0)),
                  pl.BlockSpec((cl, akv), lambda b: (b, 0))],
        out_specs=pl.BlockSpec((cl, aq), lambda b: (b, 0)),
        out_shape=jax.ShapeDtypeStruct((geom.n_ctx, aq), F32),
        compiler_params=_cparams(1),
        name="attn_ctx",
    )(sink, qa, ka, va)

    blk = ATT_BLOCK
    nb = geom.lat_len // blk
    base = geom.n_ctx // blk
    past = cache_k.shape[1]

    def cur(b, i):
        return (base + b * nb + i, 0)

    def prv(b, i):
        return (base + b * nb + jnp.maximum(i - 1, 0), 0)

    def nxt(b, i):
        return (base + b * nb + jnp.minimum(i + 1, nb - 1), 0)

    kv_spec = lambda f: pl.BlockSpec((blk, akv), f)
    cache_spec = pl.BlockSpec((None, past, akv), lambda b, i: (b, 0, 0))
    o_lat = pl.pallas_call(
        functools.partial(_attn_lat_kernel, n_blocks=nb),
        grid=(geom.n_lat_seq, nb),
        in_specs=[smem, pl.BlockSpec((blk, aq), cur),
                  kv_spec(prv), kv_spec(cur), kv_spec(nxt),
                  kv_spec(prv), kv_spec(cur), kv_spec(nxt),
                  cache_spec, cache_spec],
        out_specs=pl.BlockSpec((blk, aq), lambda b, i: (b * nb + i, 0)),
        out_shape=jax.ShapeDtypeStruct((geom.n_lat, aq), F32),
        compiler_params=_cparams(2),
        name="attn_lat",
    )(sink, qa, ka, ka, ka, va, va, va, cache_k, cache_v)
    return o_ctx, o_lat


def _gla_constants(c):
    levels = int(np.log2(c))
    i_idx = np.arange(c)[:, None]
    j_idx = np.arange(c)[None, :]
    lv = np.full((c, c), -1, np.int32)
    lv[np.arange(c), np.arange(c)] = levels
    for lvl in range(levels):
        s = c >> (lvl + 1)
        same_pair = (i_idx // (2 * s)) == (j_idx // (2 * s))
        lv[same_pair & ((i_idx % (2 * s)) >= s) & ((j_idx % (2 * s)) < s)] = lvl
    lv2 = np.stack([np.tile(lv, (1, 2)), np.tile(lv[::-1, ::-1], (1, 2))])
    tri = np.stack([j_idx <= i_idx, j_idx >= i_idx]).astype(np.float32)
    return jnp.asarray(tri, BF16), jnp.asarray(lv2)


def _gla_pair(q, k, v, la, tri, lv2, state, out, reverse):
    c = q.shape[0]
    levels = int(np.log2(c))
    groups = c // SUBLANES
    la_hi = la.astype(BF16)
    rem = la - la_hi.astype(F32)
    la_mid = rem.astype(BF16)
    la_lo = (rem - la_mid.astype(F32)).astype(BF16)
    cum = (_dot(tri, la_hi) + _dot(tri, la_mid)) + _dot(tri, la_lo)
    edge = 0 if reverse else c - 1
    total = cum[edge:edge + 1, :]
    lane = lax.broadcasted_iota(jnp.int32, (c, LANES), 1)
    row = lax.broadcasted_iota(jnp.int32, (c, LANES), 0)
    lo = lane < B_DK

    def heads_on_rows(kt):
        return jnp.concatenate([jnp.where(lo, kt, 0.0), jnp.where(lo, 0.0, kt)], axis=0).astype(BF16)

    def rows_of(x, g):
        return x[g * SUBLANES:(g + 1) * SUBLANES]

    yield
    p = _dot_nt(q.astype(BF16), heads_on_rows(k))
    a = [jnp.where(rows_of(lv2, g) == levels, rows_of(p, g), 0.0) for g in range(groups)]
    for lvl in range(levels):
        yield
        s = c >> (lvl + 1)
        if s >= SUBLANES:
            q_parts, k_parts, q_groups = [], [], []
            zeros = jnp.zeros((s, LANES), F32)
            for b0 in range(0, c, 2 * s):
                first, second = slice(b0, b0 + s), slice(b0 + s, b0 + 2 * s)
                qs, ks, m = (first, second, b0 + s) if reverse else (second, first, b0 + s - 1)
                ref = cum[m:m + 1, :]
                q_parts.append(q[qs] * jnp.exp(cum[qs] - ref))
                k_blk = k[ks] * jnp.exp(ref - cum[ks])
                k_parts += [zeros, k_blk] if reverse else [k_blk, zeros]
                q_groups += list(range(qs.start // SUBLANES, qs.stop // SUBLANES))
            qt = jnp.concatenate(q_parts, axis=0)
            kt = jnp.concatenate(k_parts, axis=0)
        else:
            q_groups = list(range(groups))
            pos = row & (2 * s - 1)
            q_side = (pos < s) if reverse else (pos >= s)
            if s == 1:
                qt = jnp.where(q_side, q * jnp.exp(la), 0.0)
                kt = jnp.where(q_side, 0.0, k)
            else:
                cum3 = cum.reshape(groups, SUBLANES, LANES)
                sub = lax.broadcasted_iota(jnp.int32, cum3.shape, 1)
                ref = None
                for b0 in range(0, SUBLANES, 2 * s):
                    m = b0 + s if reverse else b0 + s - 1
                    r = cum3[:, m:m + 1, :]
                    ref = r if ref is None else jnp.where(sub >= b0, r, ref)
                e = jnp.exp(-jnp.abs(cum3 - ref)).reshape(c, LANES)
                qt = jnp.where(q_side, q * e, 0.0)
                kt = jnp.where(q_side, 0.0, k * e)
        p = _dot_nt(qt.astype(BF16), heads_on_rows(kt))
        for idx, g in enumerate(q_groups):
            a[g] = jnp.where(rows_of(lv2, g) == lvl, rows_of(p, idx), a[g])

    yield
    a_full = jnp.concatenate(a, axis=0).astype(BF16)
    v_b = v.astype(BF16)
    intra = [_dot(a_full[:, hh * c:(hh + 1) * c], v_b[:, hh * B_DV:(hh + 1) * B_DV]) for hh in range(2)]
    upd = _dot_tn(v_b, (k * jnp.exp(total - cum)).astype(BF16))
    q_in = (q * jnp.exp(cum)).astype(BF16)
    yield
    st = state[0]
    o_inter = _dot_nt(q_in, st.astype(BF16))
    out.append(jnp.concatenate([intra[hh] + o_inter[:, hh * B_DV:(hh + 1) * B_DV] for hh in range(2)], axis=1))
    row2 = lax.broadcasted_iota(jnp.int32, st.shape, 0)
    lane2 = lax.broadcasted_iota(jnp.int32, st.shape, 1)
    state[0] = jnp.where((row2 < B_DV) == (lane2 < B_DK), jnp.exp(total) * st + upd, 0.0)


def _gla_kernel(gin_f_ref, s0_f_ref, gin_b_ref, s0_b_ref, tri_ref, lv_ref,
                o_f_ref, o_b_ref, sfin_f_ref, sfin_b_ref, st_ref, *, first_last):
    n = pl.program_id(0)
    c = GLA_CHUNK
    bqk = B_HEADS * B_DK
    bv = B_HEADS * B_DV
    pairs = B_HEADS // 2
    row2 = lax.broadcasted_iota(jnp.int32, (2 * B_DV, LANES), 0)
    lane2 = lax.broadcasted_iota(jnp.int32, (2 * B_DV, LANES), 1)
    diag_blocks = (row2 < B_DV) == (lane2 < B_DK)
    dirs = ((gin_f_ref, s0_f_ref, o_f_ref, sfin_f_ref), (gin_b_ref, s0_b_ref, o_b_ref, sfin_b_ref))
    for d, (_, s0_ref, _, _) in enumerate(dirs):
        @pl.when(first_last(d, n)[0])
        def _():
            for pr in range(pairs):
                s0t = jnp.concatenate([s0_ref[2 * pr], s0_ref[2 * pr + 1]], axis=0).T
                st_ref[d, pr] = jnp.where(diag_blocks, jnp.concatenate([s0t, s0t], axis=0), 0.0)

    n_sub = gin_f_ref.shape[0] // c
    chains, states = [], {}
    for d, (gin_ref, _, o_ref, _) in enumerate(dirs):
        la0 = 2 * bqk + bv + d * bqk
        for pr in range(pairs):
            states[d, pr] = [st_ref[d, pr]]
            for ci in (reversed(range(n_sub)) if d == 1 else range(n_sub)):
                rows = slice(ci * c, (ci + 1) * c)
                out = []
                gen = _gla_pair(gin_ref[rows, pr * LANES:(pr + 1) * LANES],
                                gin_ref[rows, bqk + pr * LANES:bqk + (pr + 1) * LANES],
                                gin_ref[rows, 2 * bqk + pr * 2 * B_DV:2 * bqk + (pr + 1) * 2 * B_DV],
                                gin_ref[rows, la0 + pr * LANES:la0 + (pr + 1) * LANES],
                                tri_ref[d], lv_ref[d], states[d, pr], out, reverse=(d == 1))
                chains.append((gen, out, o_ref, rows, pr))
    running = True
    while running:
        running = False
        for gen, _, _, _, _ in chains:
            for _ in gen:
                running = True
                break
    for _, out, o_ref, rows, pr in chains:
        o_ref[rows, pr * 2 * B_DV:(pr + 1) * 2 * B_DV] = out[0]
    for (d, pr), state in states.items():
        st_ref[d, pr] = state[0]

    for d, (_, _, _, sfin_ref) in enumerate(dirs):
        @pl.when(first_last(d, n)[1])
        def _():
            for pr in range(pairs):
                st = st_ref[d, pr]
                for hh in range(2):
                    blk = st[hh * B_DV:(hh + 1) * B_DV, :].T
                    sfin_ref[2 * pr + hh] = blk[hh * B_DK:(hh + 1) * B_DK, :]


def _gla(gin, s0_all, consts, geom):
    n = gin.shape[0]
    c = min(GLA_STEP, geom.ctx_len)
    assert c % GLA_CHUNK == 0
    tri, lv2 = consts
    n_chunks = n // c
    ctx_chunks = geom.n_ctx // c
    cpc = geom.ctx_len // c
    cpl = geom.lat_len // c
    n_seq = geom.n_ctx_seq + geom.n_lat_seq
    bv = B_HEADS * B_DV

    def bwd(nn):
        return n_chunks - 1 - nn

    def seq_of(g):
        return jnp.where(g < ctx_chunks, g // cpc, geom.n_ctx_seq + (g - ctx_chunks) // cpl)

    def first_last(d, nn):
        g = bwd(nn) if d == 1 else nn
        pos = jnp.where(g < ctx_chunks, g % cpc, (g - ctx_chunks) % cpl)
        per = jnp.where(g < ctx_chunks, cpc, cpl)
        at_start, at_end = pos == 0, pos == per - 1
        return (at_end, at_start) if d == 1 else (at_start, at_end)

    def specs(idx, d):
        return [pl.BlockSpec((c, gin.shape[1]), lambda nn: (idx(nn), 0)),
                pl.BlockSpec((None, None, B_HEADS, B_DK, B_DV), lambda nn: (seq_of(idx(nn)), d, 0, 0, 0))]

    fwd = lambda nn: nn
    sfin_shape = jax.ShapeDtypeStruct((n_seq, B_HEADS, B_DK, B_DV), F32)
    return pl.pallas_call(
        functools.partial(_gla_kernel, first_last=first_last),
        grid=(n_chunks,),
        in_specs=specs(fwd, 0) + specs(bwd, 1) + [_resident(tri.shape), _resident(lv2.shape)],
        out_specs=[pl.BlockSpec((c, bv), lambda nn: (nn, 0)),
                   pl.BlockSpec((c, bv), lambda nn: (bwd(nn), 0)),
                   pl.BlockSpec((None, B_HEADS, B_DK, B_DV), lambda nn: (seq_of(nn), 0, 0, 0)),
                   pl.BlockSpec((None, B_HEADS, B_DK, B_DV), lambda nn: (seq_of(bwd(nn)), 0, 0, 0))],
        out_shape=[jax.ShapeDtypeStruct((n, bv), F32), jax.ShapeDtypeStruct((n, bv), F32), sfin_shape, sfin_shape],
        scratch_shapes=[pltpu.VMEM((2, B_HEADS // 2, 2 * B_DV, LANES), F32)],
        compiler_params=_cparams(1),
        name="gla",
    )(gin, s0_all, gin, s0_all, tri, lv2)


def _even_out_kernel(x_ref, mod_ref, oa_ref, of_ref, ob_ref, gb_ref, gn_ref, w_ref, o_ref):
    ob = of_ref[...] + ob_ref[...]
    gb = gb_ref[...]
    gn = gn_ref[...]
    aq = oa_ref.shape[1]
    acc = _dot(oa_ref[...].astype(BF16), w_ref[0:aq, :])
    for hh in range(B_HEADS):
        blk = _rms(ob[:, hh * B_DV:(hh + 1) * B_DV], gn) * _silu(gb[:, hh * B_DV:(hh + 1) * B_DV])
        acc = acc + _dot(blk.astype(BF16), w_ref[aq + hh * B_DV:aq + (hh + 1) * B_DV, :])
    o_ref[...] = x_ref[...] + mod_ref[2:3, :] * acc


def _even_out(x, mod, layer, oa, o_f, o_b, gb, gn, w_out, geom):
    n, d = x.shape
    bv = gb.shape[1]
    return pl.pallas_call(
        _even_out_kernel,
        grid=(geom.tiles,),
        in_specs=[_row_spec(geom, d), _mod_spec(geom, layer, d), _row_spec(geom, oa.shape[1]),
                  _row_spec(geom, bv), _row_spec(geom, bv), _row_spec(geom, bv),
                  _resident(gn.shape), _resident(w_out.shape)],
        out_specs=_row_spec(geom, d),
        out_shape=jax.ShapeDtypeStruct((n, d), F32),
        compiler_params=_cparams(1),
        name="even_out",
    )(x, mod, oa, o_f, o_b, gb, gn, w_out)


def _odd_in_kernel(xp_ref, x_ref, xn_ref, mod_ref, g_ref, w_ref, cw_ref, cb_ref, gy_ref, xr_ref, s_ref, *, geom):
    i = pl.program_id(0)
    t = geom.tile
    first, last = geom.seq_edges(i)
    h_ext = _modulated_ext(xp_ref, x_ref, xn_ref, g_ref, mod_ref[0:1, :], mod_ref[1:2, :], first, last)
    yx_ext = _dot(h_ext, w_ref[...])
    d_rnn = gy_ref.shape[1]
    gy_ref[...] = jax.nn.gelu(yx_ext[HALO:HALO + t, 0:d_rnn])
    left = CONV_W // 2
    for sl in range(d_rnn // LANES):
        cols = slice(sl * LANES, (sl + 1) * LANES)
        s_ref[sl] = yx_ext[:, d_rnn + sl * LANES:d_rnn + (sl + 1) * LANES]
        xr = cb_ref[:, cols]
        for tap in range(CONV_W):
            xr = xr + s_ref[sl, pl.ds(HALO - left + tap, t), :] * cw_ref[tap:tap + 1, cols]
        xr_ref[:, cols] = xr


def _odd_in(x, mod, layer, g, w_in, cw, cb, geom):
    n, d = x.shape
    t = geom.tile
    d_rnn = w_in.shape[1] // 2
    return pl.pallas_call(
        functools.partial(_odd_in_kernel, geom=geom),
        grid=(geom.tiles,),
        in_specs=_halo_specs(geom, d) + [_mod_spec(geom, layer, d)] +
                 [_resident(a.shape) for a in (g, w_in, cw, cb)],
        out_specs=[_row_spec(geom, d_rnn), _row_spec(geom, d_rnn)],
        out_shape=[jax.ShapeDtypeStruct((n, d_rnn), F32), jax.ShapeDtypeStruct((n, d_rnn), F32)],
        scratch_shapes=[pltpu.VMEM((d_rnn // LANES, t + 2 * HALO, LANES), F32)],
        compiler_params=_cparams(1),
        name="odd_in",
    )(x, x, x, mod, g, w_in, cw, cb)


def _scan_slab(a, u, carry, reverse):
    rows = a.shape[0]
    groups = rows // SUBLANES
    a3 = a.reshape(groups, SUBLANES, LANES)
    u3 = u.reshape(groups, SUBLANES, LANES)
    sub = lax.broadcasted_iota(jnp.int32, a3.shape, 1)
    k = 1
    while k < SUBLANES:
        shift = SUBLANES - k if reverse else k
        ok = sub < SUBLANES - k if reverse else sub >= k
        a_s = jnp.where(ok, pltpu.roll(a3, shift, 1), 1.0)
        u_s = jnp.where(ok, pltpu.roll(u3, shift, 1), 0.0)
        u3 = a3 * u_s + u3
        a3 = a3 * a_s
        k *= 2
    hs = [None] * groups
    edge = 0 if reverse else SUBLANES - 1
    for gi in (reversed(range(groups)) if reverse else range(groups)):
        h = a3[gi] * carry + u3[gi]
        hs[gi] = h
        carry = h[edge:edge + 1, :]
    return jnp.concatenate(hs, axis=0), carry


def _odd_scan_kernel(xr_ref, wg_ref, bg_ref, lam_ref, h0_ref, h_ref, carry_ref, *, first_of, reverse):
    n = pl.program_id(0)

    @pl.when(first_of(n))
    def _():
        carry_ref[...] = h0_ref[...]

    xr = xr_ref[...]
    xr_b = xr.astype(BF16)
    log_sig_lam = jax.nn.log_sigmoid(lam_ref[...])
    n_grp, grp = wg_ref.shape[0], wg_ref.shape[1]
    for gi in range(n_grp):
        cols = slice(gi * grp, (gi + 1) * grp)
        ri = _dot(xr_b[:, cols], wg_ref[gi]) + bg_ref[gi]
        r_gate = jax.nn.sigmoid(ri[:, :grp])
        i_gate = jax.nn.sigmoid(ri[:, grp:])
        log_a = RG_C * r_gate * log_sig_lam[:, cols]
        a = jnp.exp(log_a)
        u = jnp.sqrt(-jnp.tanh(log_a) * (a * a + 1.0)) * (i_gate * xr[:, cols])
        for sl in range(grp // LANES):
            lanes = slice(gi * grp + sl * LANES, gi * grp + (sl + 1) * LANES)
            h, carry = _scan_slab(a[:, sl * LANES:(sl + 1) * LANES], u[:, sl * LANES:(sl + 1) * LANES],
                                  carry_ref[:, lanes], reverse)
            h_ref[:, lanes] = h
            carry_ref[:, lanes] = carry


def _odd_scan(xr, wg, bg, lam, h0_all, geom, direction):
    n, d = xr.shape
    ts = SCAN_TILE
    n_tiles = n // ts
    ctx_tiles = geom.n_ctx // ts
    tpc = geom.ctx_len // ts
    tpl = geom.lat_len // ts
    reverse = direction == 1

    def tile(nn):
        return n_tiles - 1 - nn if reverse else nn

    def seq_of(g):
        return jnp.where(g < ctx_tiles, g // tpc, geom.n_ctx_seq + (g - ctx_tiles) // tpl)

    def first_of(nn):
        g = tile(nn)
        pos = jnp.where(g < ctx_tiles, g % tpc, (g - ctx_tiles) % tpl)
        per = jnp.where(g < ctx_tiles, tpc, tpl)
        return pos == per - 1 if reverse else pos == 0

    blk = pl.BlockSpec((ts, d), lambda nn: (tile(nn), 0))
    return pl.pallas_call(
        functools.partial(_odd_scan_kernel, first_of=first_of, reverse=reverse),
        grid=(n_tiles,),
        in_specs=[blk, _resident(wg.shape), _resident(bg.shape), _resident(lam.shape),
                  pl.BlockSpec((None, None, 1, d), lambda nn: (seq_of(tile(nn)), direction, 0, 0))],
        out_specs=blk,
        out_shape=jax.ShapeDtypeStruct((n, d), F32),
        scratch_shapes=[pltpu.VMEM((1, d), F32)],
        compiler_params=_cparams(1),
        name="odd_scan",
    )(xr, wg, bg, lam, h0_all)


def _odd_out_kernel(x_ref, mod_ref, gy_ref, hf_ref, hb_ref, w_ref, o_ref):
    z = gy_ref[...] * (hf_ref[...] + hb_ref[...])
    o_ref[...] = x_ref[...] + mod_ref[2:3, :] * _dot(z.astype(BF16), w_ref[...])


def _odd_out(x, mod, layer, gy, h_f, h_b, w_out, geom):
    n, d = x.shape
    dr = gy.shape[1]
    return pl.pallas_call(
        _odd_out_kernel,
        grid=(geom.tiles,),
        in_specs=[_row_spec(geom, d), _mod_spec(geom, layer, d), _row_spec(geom, dr),
                  _row_spec(geom, dr), _row_spec(geom, dr), _resident(w_out.shape)],
        out_specs=_row_spec(geom, d),
        out_shape=jax.ShapeDtypeStruct((n, d), F32),
        compiler_params=_cparams(1),
        name="odd_out",
    )(x, mod, gy, h_f, h_b, w_out)


def _modulated_ext(xp_ref, x_ref, xn_ref, g_ref, shift, scale, first, last):
    g = g_ref[...]
    h_p = _rms(xp_ref[...], g) * (1.0 + scale) + shift
    h_c = _rms(x_ref[...], g) * (1.0 + scale) + shift
    h_n = _rms(xn_ref[...], g) * (1.0 + scale) + shift
    h_p = jnp.where(first, 0.0, h_p)
    h_n = jnp.where(last, 0.0, h_n)
    return jnp.concatenate([h_p, h_c, h_n], axis=0).astype(BF16)


def _ffn_kernel(xp_ref, x_ref, xn_ref, mod_ref, g_ref, wu_ref, cw_ref, cb_ref, wd_ref, fn_ref, o_ref, s_ref,
                *, geom, final):
    i = pl.program_id(0)
    t = geom.tile
    first, last = geom.seq_edges(i)
    h_ext = _modulated_ext(xp_ref, x_ref, xn_ref, g_ref, mod_ref[3:4, :], mod_ref[4:5, :], first, last)
    n_chunks = wd_ref.shape[0]
    n_slab = wu_ref.shape[3] // LANES
    def project(c):
        for part in range(2):
            u_ext = _dot(h_ext, wu_ref[part, c])
            for sl in range(n_slab):
                s_ref[c % 2, part * n_slab + sl] = u_ext[:, sl * LANES:(sl + 1) * LANES]

    acc = None
    project(0)
    for c in range(n_chunks):
        if c + 1 < n_chunks:
            project(c + 1)
        acts = []
        for sl in range(n_slab):
            cols = slice(sl * LANES, (sl + 1) * LANES)
            halves = []
            for part in range(2):
                slab = s_ref.at[c % 2, part * n_slab + sl]
                cw = cw_ref[part, c]
                halves.append(slab[pl.ds(HALO - 1, t), :] * cw[0:1, cols]
                              + slab[pl.ds(HALO, t), :] * cw[1:2, cols]
                              + slab[pl.ds(HALO + 1, t), :] * cw[2:3, cols]
                              + cb_ref[part, c][:, cols])
            acts.append((_silu(halves[0]) * halves[1]).astype(BF16))
        part_out = _dot(jnp.concatenate(acts, axis=1), wd_ref[c])
        acc = part_out if acc is None else acc + part_out
    y = x_ref[...] + mod_ref[5:6, :] * acc
    if final:
        y = _rms(y, fn_ref[...])
    o_ref[...] = y


def _ffn(x, mod, layer, g, wu, cw, cb, wd, fn, geom, final):
    n, d = x.shape
    n_slab = wu.shape[3] // LANES
    return pl.pallas_call(
        functools.partial(_ffn_kernel, geom=geom, final=final),
        grid=(geom.tiles,),
        in_specs=_halo_specs(geom, d) + [_mod_spec(geom, layer, d)] +
                 [_resident(a.shape) for a in (g, wu, cw, cb, wd, fn)],
        out_specs=_row_spec(geom, d),
        out_shape=jax.ShapeDtypeStruct((n, d), F32),
        scratch_shapes=[pltpu.VMEM((2, 2 * n_slab, geom.tile + 2 * HALO, LANES), F32)],
        compiler_params=_cparams(1),
        name="ffn",
    )(x, x, x, mod, g, wu, cw, cb, wd, fn)


def _rope_tables(t_len):
    rows = t_len // GRID_W
    row = jnp.repeat(jnp.arange(rows, dtype=F32), GRID_W)
    col = jnp.tile(jnp.arange(GRID_W, dtype=F32), rows)
    n_freq = HEAD_DIM // 4
    inv_freq = jnp.power(ROPE_BASE, -jnp.arange(n_freq, dtype=F32) / n_freq)
    ang_r = row[:, None] * inv_freq
    ang_c = col[:, None] * inv_freq
    cos_h = jnp.concatenate([jnp.cos(ang_r)] * 2 + [jnp.cos(ang_c)] * 2, axis=1)
    sin_h = jnp.concatenate([-jnp.sin(ang_r), jnp.sin(ang_r), -jnp.sin(ang_c), jnp.sin(ang_c)], axis=1)
    reps = LANES // HEAD_DIM
    return jnp.tile(cos_h, (1, reps)), jnp.tile(sin_h, (1, reps))


def _block_diag_groups(w, grp):
    nb, bw, _ = w.shape
    per = grp // bw
    w = w.reshape(nb // per, per, bw, bw)
    eye = jnp.eye(per, dtype=w.dtype)
    return jnp.einsum('gpij,pq->gpiqj', w, eye).reshape(nb // per, grp, grp)


def kernel(x_prompt, x_sample, cache_attn_k, cache_attn_v, state_gla, state_rglru, c, c_ctx,
           norm_mix, norm_ffn, w_ada, b_ada,
           ev_w_in, ev_sink, ev_w_gate_f, ev_b_gate_f, ev_w_gate_b, ev_b_gate_b, ev_gla_norm, ev_w_out,
           od_w_in, od_conv_w, od_conv_b, od_w_a, od_b_a, od_w_i, od_b_i, od_lambda, od_w_out,
           ffn_w_up, ffn_conv_w, ffn_conv_b, ffn_w_down, final_norm):
    batch, seq, d = x_prompt.shape
    dec_batch, dec_seq, _ = x_sample.shape
    depth = w_ada.shape[0]
    n_even = ev_w_in.shape[0]
    geom = _Geom(batch, seq, dec_batch, dec_seq, min(ROW_TILE, dec_seq))
    geom_c = _Geom(batch, seq, dec_batch, dec_seq, min(CONV_TILE, seq))
    n_seq = batch + dec_batch

    cond = jnp.concatenate([c_ctx[None, :], c], axis=0)
    cond = jnp.pad(cond, ((0, SUBLANES - cond.shape[0] % SUBLANES), (0, 0)))
    mod = _ada(cond, w_ada, b_ada)

    x = jnp.concatenate([x_prompt.reshape(batch * seq, d), x_sample.reshape(dec_batch * dec_seq, d)], axis=0)
    cos_t, sin_t = _rope_tables(dec_seq)
    gla_consts = _gla_constants(GLA_CHUNK)
    akv = A_KV_HEADS * HEAD_DIM
    bqk = B_HEADS * B_DK
    d_ff = ffn_w_down.shape[1]
    n_ff = d_ff // FF_CHUNK

    new_k, new_v, new_gla, new_rg = [], [], [], []
    for layer in range(depth):
        j = layer // 2
        g_mix = norm_mix[layer][None, :]
        if layer % 2 == 0:
            w_in = ev_w_in[j]
            main = w_in.shape[1] - 2 * GATE_RANK
            w_in_p = jnp.pad(w_in, ((0, 0), (0, LANES - 2 * GATE_RANK))).astype(BF16)
            wg = jnp.zeros((LANES, 2 * bqk), F32)
            wg = wg.at[0:GATE_RANK, 0:bqk].set(ev_w_gate_f[j])
            wg = wg.at[GATE_RANK:2 * GATE_RANK, bqk:2 * bqk].set(ev_w_gate_b[j]).astype(BF16)
            bg = jnp.concatenate([ev_b_gate_f[j], ev_b_gate_b[j]])[None, :]
            assert main % LANES == 0
            qa, ka, va, gb, gin = _even_in(x, mod, layer, g_mix, w_in_p, wg, bg, cos_t, sin_t, geom)
            ck = cache_attn_k[:, j].reshape(dec_batch, -1, akv)
            cv = cache_attn_v[:, j].reshape(dec_batch, -1, akv)
            o_ctx, o_lat = _attention(qa, ka, va, ev_sink[j], ck, cv, geom)
            oa = jnp.concatenate([o_ctx, o_lat], axis=0)
            s0_all = jnp.concatenate([jnp.zeros((batch,) + state_gla.shape[2:], F32), state_gla[:, j]], axis=0)
            o_f, o_b, s_f, s_b = _gla(gin, s0_all, gla_consts, geom)
            x = _even_out(x, mod, layer, oa, o_f, o_b, gb, ev_gla_norm[j][None, :], ev_w_out[j].astype(BF16), geom)
            new_k.append(ka[:geom.n_ctx].reshape(batch, seq, A_KV_HEADS, HEAD_DIM))
            new_v.append(va[:geom.n_ctx].reshape(batch, seq, A_KV_HEADS, HEAD_DIM))
            new_gla.append(jnp.stack([s_f[:batch], s_b[:batch]], axis=1))
        else:
            w_in = od_w_in[j].astype(BF16)
            d_rnn = w_in.shape[1] // 2
            grp = 2 * LANES
            wa = jnp.stack([_block_diag_groups(od_w_a[j, dr], grp) for dr in range(2)])
            wi = jnp.stack([_block_diag_groups(od_w_i[j, dr], grp) for dr in range(2)])
            wgate = jnp.concatenate([wa, wi], axis=-1).astype(BF16)
            ba = od_b_a[j].reshape(2, d_rnn // grp, 1, grp)
            bi = od_b_i[j].reshape(2, d_rnn // grp, 1, grp)
            bgate = jnp.concatenate([ba, bi], axis=-1)
            gy, xr = _odd_in(x, mod, layer, g_mix, w_in, od_conv_w[j], od_conv_b[j][None, :], geom_c)
            h0_all = jnp.concatenate([jnp.zeros((batch, 2, d_rnn), F32), state_rglru[:, j]], axis=0)
            h0_all = h0_all.reshape(n_seq, 2, 1, d_rnn)
            h_f, h_b = [_odd_scan(xr, wgate[dr], bgate[dr], od_lambda[j, dr][None, :], h0_all, geom, dr)
                        for dr in range(2)]
            x = _odd_out(x, mod, layer, gy, h_f, h_b, od_w_out[j].astype(BF16), geom)
            new_rg.append(jnp.stack([h_f[:geom.n_ctx].reshape(batch, seq, d_rnn)[:, -1],
                                     h_b[:geom.n_ctx].reshape(batch, seq, d_rnn)[:, 0]], axis=1))
        wu = ffn_w_up[layer].astype(BF16).reshape(d, 2, n_ff, FF_CHUNK).transpose(1, 2, 0, 3)
        cw = ffn_conv_w[layer].reshape(FFN_CONV_W, 2, n_ff, FF_CHUNK).transpose(1, 2, 0, 3)
        cb = ffn_conv_b[layer].reshape(2, n_ff, 1, FF_CHUNK)
        wd = ffn_w_down[layer].astype(BF16).reshape(n_ff, FF_CHUNK, d)
        x = _ffn(x, mod, layer, norm_ffn[layer][None, :], wu, cw, cb, wd, final_norm[None, :], geom_c,
                 final=(layer == depth - 1))

    y_prompt = x[:geom.n_ctx].reshape(batch, seq, d)
    y_sample = x[geom.n_ctx:].reshape(dec_batch, dec_seq, d)
    return (y_prompt, y_sample, jnp.stack(new_k, axis=1), jnp.stack(new_v, axis=1),
            jnp.stack(new_gla, axis=1), jnp.stack(new_rg, axis=1))
```

```python
import functools

import jax
import jax.numpy as jnp
import numpy as np
from jax import lax
from jax.experimental import pallas as pl
from jax.experimental.pallas import tpu as pltpu

F32 = jnp.float32
BF16 = jnp.bfloat16

EPS = 1e-6
NEG_INF = -1e30
GRID_W = 64
A_HEADS = 8
A_KV_HEADS = 2
HEAD_DIM = 64
WINDOW = 128
ATT_BLOCK = 128
ROPE_BASE = 10000.0
B_HEADS = 4
B_DK = 64
B_DV = 128
GATE_RANK = 16
GATE_NORMALIZER = 16.0
RG_BLOCKS = 16
RG_C = 8.0
CONV_W = 4
FFN_CONV_W = 3

LANES = 128
SUBLANES = 8
HALO = SUBLANES
ROW_TILE = 512
CONV_TILE = 256
GLA_CHUNK = 128
GLA_STEP = 256
SCAN_TILE = 256
SCAN_COLS = 512
FF_CHUNK = 256
VMEM_LIMIT = 56 * 1024 * 1024


def _cparams(n_axes):
    return pltpu.CompilerParams(dimension_semantics=("arbitrary",) * n_axes,
                                vmem_limit_bytes=VMEM_LIMIT)


def _resident(shape):
    nd = len(shape)
    return pl.BlockSpec(shape, lambda *_: (0,) * nd, pipeline_mode=pl.Buffered(1))


def _silu(x):
    return x * jax.nn.sigmoid(x)


def _rms(x, g):
    return x * lax.rsqrt(jnp.mean(x * x, axis=-1, keepdims=True) + EPS) * g


def _dot(a, b):
    return jnp.dot(a, b, preferred_element_type=F32)


def _dot_nt(a, b):
    return lax.dot_general(a, b, (((1,), (1,)), ((), ())), preferred_element_type=F32)


def _dot_tn(a, b):
    return lax.dot_general(a, b, (((0,), (0,)), ((), ())), preferred_element_type=F32)


def _ada_kernel(c_ref, w_ref, b_ref, o_ref):
    s = _silu(c_ref[...])
    o_ref[...] = _dot(s.astype(BF16), w_ref[...].astype(BF16)) + b_ref[...]


def _ada(cond, w_ada, b_ada):
    depth, d, d6 = w_ada.shape
    nb = 4
    cb = d6 // nb
    rows = cond.shape[0]
    out = pl.pallas_call(
        _ada_kernel,
        grid=(depth, nb),
        in_specs=[pl.BlockSpec((rows, d), lambda l, j: (0, 0)),
                  pl.BlockSpec((None, d, cb), lambda l, j: (l, 0, j)),
                  pl.BlockSpec((None, 1, cb), lambda l, j: (l, 0, j))],
        out_specs=pl.BlockSpec((None, rows, cb), lambda l, j: (l, 0, j)),
        out_shape=jax.ShapeDtypeStruct((depth, rows, d6), F32),
        compiler_params=_cparams(2),
        name="ada",
    )(cond, w_ada, b_ada.reshape(depth, 1, d6))
    return out.reshape(depth, rows, 6, d)


class _Geom:
    def __init__(self, n_ctx_seq, ctx_len, n_lat_seq, lat_len, tile):
        assert ctx_len % tile == 0 or tile % ctx_len == 0
        assert lat_len % tile == 0
        self.n_ctx_seq, self.ctx_len, self.n_lat_seq, self.lat_len = n_ctx_seq, ctx_len, n_lat_seq, lat_len
        self.tile = tile
        self.n_ctx = n_ctx_seq * ctx_len
        self.n_lat = n_lat_seq * lat_len
        self.n = self.n_ctx + self.n_lat
        assert self.n_ctx % tile == 0
        self.ctx_tiles = self.n_ctx // tile
        self.lat_tiles_per_seq = lat_len // tile
        self.tiles = self.n // tile

    def mod_row(self, i):
        return jnp.where(i < self.ctx_tiles, 0, 1 + (i - self.ctx_tiles) // self.lat_tiles_per_seq)

    def lat_tile(self, i):
        return jnp.where(i < self.ctx_tiles, 0, (i - self.ctx_tiles) % self.lat_tiles_per_seq)

    def seq_edges(self, i):
        assert self.ctx_len % self.tile == 0
        is_ctx = i < self.ctx_tiles
        seq_len = jnp.where(is_ctx, self.ctx_len, self.lat_len)
        start = (i * self.tile - jnp.where(is_ctx, 0, self.n_ctx)) % seq_len
        return start == 0, start + self.tile == seq_len

    def seq_pos(self, i, rows, offset=0):
        r = lax.broadcasted_iota(jnp.int32, (rows, 1), 0) + (i * self.tile + offset)
        is_ctx = i < self.ctx_tiles
        seq_len = jnp.where(is_ctx, self.ctx_len, self.lat_len)
        base = jnp.where(is_ctx, 0, self.n_ctx)
        return (r - base) % seq_len, seq_len


def _halo_specs(geom, d):
    t = geom.tile
    per = t // HALO
    last = geom.n // HALO - 1
    return [pl.BlockSpec((HALO, d), lambda i: (jnp.maximum(i * per - 1, 0), 0)),
            pl.BlockSpec((t, d), lambda i: (i, 0)),
            pl.BlockSpec((HALO, d), lambda i: (jnp.minimum((i + 1) * per, last), 0))]


def _mod_spec(geom, layer, d):
    return pl.BlockSpec((None, None, 6, d), lambda i: (layer, geom.mod_row(i), 0, 0))


def _row_spec(geom, width, col=0):
    return pl.BlockSpec((geom.tile, width), lambda i: (i, col))


def _shift_rows(x_ext, shift, rows):
    if shift == 0:
        return x_ext[HALO:HALO + rows]
    return pltpu.roll(x_ext, shift % x_ext.shape[0], 0)[HALO:HALO + rows]


def _rope_apply(t, cos, sin):
    lane = lax.broadcasted_iota(jnp.int32, t.shape, 1)
    swapped = jnp.where((lane & 16) == 0, pltpu.roll(t, LANES - 16, 1), pltpu.roll(t, 16, 1))
    return t * cos + swapped * sin


def _even_in_kernel(x_ref, mod_ref, g_ref, w_ref, wg_ref, bg_ref, cos_ref, sin_ref,
                    qa_ref, ka_ref, va_ref, gb_ref, gin_ref, *, geom):
    i = pl.program_id(0)
    h = _rms(x_ref[...], g_ref[...]) * (1.0 + mod_ref[1:2, :]) + mod_ref[0:1, :]
    p = _dot(h.astype(BF16), w_ref[...])
    aq = A_HEADS * HEAD_DIM
    akv = A_KV_HEADS * HEAD_DIM
    bqk = B_HEADS * B_DK
    bv = B_HEADS * B_DV
    o = 0
    q_a = p[:, o:o + aq] * (HEAD_DIM ** -0.5); o += aq
    k_a = p[:, o:o + akv]; o += akv
    va_ref[...] = p[:, o:o + akv]; o += akv
    gin_ref[:, 0:bqk] = p[:, o:o + bqk] * (B_DK ** -0.5); o += bqk
    gin_ref[:, bqk:2 * bqk] = p[:, o:o + bqk]; o += bqk
    gin_ref[:, 2 * bqk:2 * bqk + bv] = p[:, o:o + bv]; o += bv
    gb_ref[...] = p[:, o:o + bv]; o += bv
    r = p[:, o:o + LANES]
    z = _dot(r.astype(BF16), wg_ref[...]) + bg_ref[...]
    gin_ref[:, 2 * bqk + bv:4 * bqk + bv] = jax.nn.log_sigmoid(z) * (1.0 / GATE_NORMALIZER)

    is_lat = i >= geom.ctx_tiles
    cos, sin = cos_ref[...], sin_ref[...]
    for j in range(aq // LANES):
        blk = q_a[:, j * LANES:(j + 1) * LANES]
        qa_ref[:, j * LANES:(j + 1) * LANES] = jnp.where(is_lat, _rope_apply(blk, cos, sin), blk)
    for j in range(akv // LANES):
        blk = k_a[:, j * LANES:(j + 1) * LANES]
        ka_ref[:, j * LANES:(j + 1) * LANES] = jnp.where(is_lat, _rope_apply(blk, cos, sin), blk)


def _even_in(x, mod, layer, g, w_in_p, wg, bg, cos_t, sin_t, geom):
    n, d = x.shape
    t = geom.tile
    aq = A_HEADS * HEAD_DIM
    akv = A_KV_HEADS * HEAD_DIM
    bqk = B_HEADS * B_DK
    bv = B_HEADS * B_DV
    widths = (aq, akv, akv, bv, 4 * bqk + bv)
    return pl.pallas_call(
        functools.partial(_even_in_kernel, geom=geom),
        grid=(geom.tiles,),
        in_specs=[_row_spec(geom, d), _mod_spec(geom, layer, d), _resident(g.shape),
                  _resident(w_in_p.shape), _resident(wg.shape), _resident(bg.shape),
                  pl.BlockSpec((t, LANES), lambda i: (geom.lat_tile(i), 0)),
                  pl.BlockSpec((t, LANES), lambda i: (geom.lat_tile(i), 0))],
        out_specs=[_row_spec(geom, w) for w in widths],
        out_shape=[jax.ShapeDtypeStruct((n, w), F32) for w in widths],
        compiler_params=_cparams(1),
        name="even_in",
    )(x, mod, g, w_in_p, wg, bg, cos_t, sin_t)


def _attend(q, k, v, mask_fn, sink_ref, o_ref):
    nq, nk = q.shape[0], k.shape[0]
    lane = lax.broadcasted_iota(jnp.int32, k.shape, 1)
    lo = lane < HEAD_DIM
    k_sw = pltpu.roll(k, HEAD_DIM, 1)
    v_sw = pltpu.roll(v, HEAD_DIM, 1)
    kk = [[jnp.where(lo, k, 0.0).astype(BF16), jnp.where(lo, 0.0, k_sw).astype(BF16)],
          [jnp.where(lo, k_sw, 0.0).astype(BF16), jnp.where(lo, 0.0, k).astype(BF16)]]
    vv = [[jnp.where(lo, v, 0.0).astype(BF16), jnp.where(lo, 0.0, v_sw).astype(BF16)],
          [jnp.where(lo, v_sw, 0.0).astype(BF16), jnp.where(lo, 0.0, v).astype(BF16)]]
    group = A_HEADS // A_KV_HEADS
    mask = None
    if mask_fn is not None:
        mask = mask_fn(lax.broadcasted_iota(jnp.int32, (nq, nk), 0), lax.broadcasted_iota(jnp.int32, (nq, nk), 1))
    heads = range(A_HEADS)
    qb = [q[:, pair * LANES:(pair + 1) * LANES].astype(BF16) for pair in range(A_HEADS // 2)]
    s = [_dot_nt(qb[h // 2], kk[h // group][h % 2]) for h in heads]
    if mask is not None:
        s = [jnp.where(mask, sh, NEG_INF) for sh in s]
    m = [jnp.maximum(jnp.max(s[h], axis=1, keepdims=True), sink_ref[h]) for h in heads]
    e = [jnp.exp(s[h] - m[h]) for h in heads]
    inv = [1.0 / (jnp.sum(e[h], axis=1, keepdims=True) + jnp.exp(sink_ref[h] - m[h])) for h in heads]
    o = [_dot((e[h] * inv[h]).astype(BF16), vv[h // group][h % 2]) for h in heads]
    for pair in range(A_HEADS // 2):
        o_ref[:, pair * LANES:(pair + 1) * LANES] = o[2 * pair] + o[2 * pair + 1]


def _attn_ctx_kernel(sink_ref, q_ref, k_ref, v_ref, o_ref):
    _attend(q_ref[...], k_ref[...], v_ref[...], None, sink_ref, o_ref)


def _attn_lat_kernel(sink_ref, q_ref, kp_ref, kc_ref, kn_ref, vp_ref, vc_ref, vn_ref, kx_ref, vx_ref, o_ctx_ref,
                     o_ref, *, n_blocks):
    del o_ctx_ref
    i = pl.program_id(1)
    blk = ATT_BLOCK
    k = jnp.concatenate([kp_ref[...], kc_ref[...], kn_ref[...], kx_ref[...]], axis=0)
    v = jnp.concatenate([vp_ref[...], vc_ref[...], vn_ref[...], vx_ref[...]], axis=0)

    def mask_fn(qi, kj):
        in_win = (jnp.abs(kj - blk - qi) <= WINDOW) & (kj < 3 * blk)
        in_win = in_win & ((kj >= blk) | (i > 0)) & ((kj < 2 * blk) | (i < n_blocks - 1))
        return in_win | (kj >= 3 * blk)

    _attend(q_ref[...], k, v, mask_fn, sink_ref, o_ref)


def _attention(qa, ka, va, sink, cache_k, cache_v, geom):
    aq = qa.shape[1]
    akv = ka.shape[1]
    smem = pl.BlockSpec(memory_space=pltpu.SMEM)
    cl = geom.ctx_len
    o_ctx = pl.pallas_call(
        _attn_ctx_kernel,
        grid=(geom.n_ctx_seq,),
        in_specs=[smem,
                  pl.BlockSpec((cl, aq), lambda b: (b, 0)),
                  pl.BlockSpec((cl, akv), lambda b: (b, 0)),
                  pl.BlockSpec((cl, akv), lambda b: (b, 0))],
        out_specs=pl.BlockSpec((cl, aq), lambda b: (b, 0)),
        out_shape=jax.ShapeDtypeStruct((geom.n, aq), F32),
        compiler_params=_cparams(1),
        name="attn_ctx",
    )(sink, qa, ka, va)

    blk = ATT_BLOCK
    nb = geom.lat_len // blk
    base = geom.n_ctx // blk
    past = cache_k.shape[1]

    def cur(b, i):
        return (base + b * nb + i, 0)

    def prv(b, i):
        return (base + b * nb + jnp.maximum(i - 1, 0), 0)

    def nxt(b, i):
        return (base + b * nb + jnp.minimum(i + 1, nb - 1), 0)

    kv_spec = lambda f: pl.BlockSpec((blk, akv), f)
    cache_spec = pl.BlockSpec((None, past, akv), lambda b, i: (b, 0, 0))
    operands = (sink, qa, ka, ka, ka, va, va, va, cache_k, cache_v, o_ctx)
    return pl.pallas_call(
        functools.partial(_attn_lat_kernel, n_blocks=nb),
        grid=(geom.n_lat_seq, nb),
        in_specs=[smem, pl.BlockSpec((blk, aq), cur),
                  kv_spec(prv), kv_spec(cur), kv_spec(nxt),
                  kv_spec(prv), kv_spec(cur), kv_spec(nxt),
                  cache_spec, cache_spec, pl.BlockSpec(memory_space=pl.ANY)],
        out_specs=pl.BlockSpec((blk, aq), cur),
        out_shape=jax.ShapeDtypeStruct((geom.n, aq), F32),
        input_output_aliases={len(operands) - 1: 0},
        compiler_params=_cparams(2),
        name="attn_lat",
    )(*operands)


def _gla_constants(c):
    levels = int(np.log2(c))
    i_idx = np.arange(c)[:, None]
    j_idx = np.arange(c)[None, :]
    lv = np.full((c, c), -1, np.int32)
    lv[np.arange(c), np.arange(c)] = levels
    for lvl in range(levels):
        s = c >> (lvl + 1)
        same_pair = (i_idx // (2 * s)) == (j_idx // (2 * s))
        lv[same_pair & ((i_idx % (2 * s)) >= s) & ((j_idx % (2 * s)) < s)] = lvl
    lv2 = np.stack([np.tile(lv, (1, 2)), np.tile(lv[::-1, ::-1], (1, 2))])
    tri = np.stack([j_idx <= i_idx, j_idx >= i_idx]).astype(np.float32)
    return jnp.asarray(tri, BF16), jnp.asarray(lv2)


def _gla_pair(q, k, v, la, tri, lv2, state, out, reverse):
    c = q.shape[0]
    levels = int(np.log2(c))
    groups = c // SUBLANES
    la_hi = la.astype(BF16)
    rem = la - la_hi.astype(F32)
    la_mid = rem.astype(BF16)
    la_lo = (rem - la_mid.astype(F32)).astype(BF16)
    cum = (_dot(tri, la_hi) + _dot(tri, la_mid)) + _dot(tri, la_lo)
    edge = 0 if reverse else c - 1
    total = cum[edge:edge + 1, :]
    lane = lax.broadcasted_iota(jnp.int32, (c, LANES), 1)
    row = lax.broadcasted_iota(jnp.int32, (c, LANES), 0)
    lo = lane < B_DK

    def heads_on_rows(kt):
        return jnp.concatenate([jnp.where(lo, kt, 0.0), jnp.where(lo, 0.0, kt)], axis=0).astype(BF16)

    def rows_of(x, g):
        return x[g * SUBLANES:(g + 1) * SUBLANES]

    yield
    p = _dot_nt(q.astype(BF16), heads_on_rows(k))
    a = [jnp.where(rows_of(lv2, g) == levels, rows_of(p, g), 0.0) for g in range(groups)]
    for lvl in range(levels):
        yield
        s = c >> (lvl + 1)
        if s >= SUBLANES:
            q_parts, k_parts, q_groups = [], [], []
            zeros = jnp.zeros((s, LANES), F32)
            for b0 in range(0, c, 2 * s):
                first, second = slice(b0, b0 + s), slice(b0 + s, b0 + 2 * s)
                qs, ks, m = (first, second, b0 + s) if reverse else (second, first, b0 + s - 1)
                ref = cum[m:m + 1, :]
                q_parts.append(q[qs] * jnp.exp(cum[qs] - ref))
                k_blk = k[ks] * jnp.exp(ref - cum[ks])
                k_parts += [zeros, k_blk] if reverse else [k_blk, zeros]
                q_groups += list(range(qs.start // SUBLANES, qs.stop // SUBLANES))
            qt = jnp.concatenate(q_parts, axis=0)
            kt = jnp.concatenate(k_parts, axis=0)
        else:
            q_groups = list(range(groups))
            pos = row & (2 * s - 1)
            q_side = (pos < s) if reverse else (pos >= s)
            if s == 1:
                qt = jnp.where(q_side, q * jnp.exp(la), 0.0)
                kt = jnp.where(q_side, 0.0, k)
            else:
                cum3 = cum.reshape(groups, SUBLANES, LANES)
                sub = lax.broadcasted_iota(jnp.int32, cum3.shape, 1)
                ref = None
                for b0 in range(0, SUBLANES, 2 * s):
                    m = b0 + s if reverse else b0 + s - 1
                    r = cum3[:, m:m + 1, :]
                    ref = r if ref is None else jnp.where(sub >= b0, r, ref)
                e = jnp.exp(-jnp.abs(cum3 - ref)).reshape(c, LANES)
                qt = jnp.where(q_side, q * e, 0.0)
                kt = jnp.where(q_side, 0.0, k * e)
        p = _dot_nt(qt.astype(BF16), heads_on_rows(kt))
        for idx, g in enumerate(q_groups):
            a[g] = jnp.where(rows_of(lv2, g) == lvl, rows_of(p, idx), a[g])

    yield
    a_full = jnp.concatenate(a, axis=0).astype(BF16)
    v_b = v.astype(BF16)
    intra = [_dot(a_full[:, hh * c:(hh + 1) * c], v_b[:, hh * B_DV:(hh + 1) * B_DV]) for hh in range(2)]
    upd = _dot_tn(v_b, (k * jnp.exp(total - cum)).astype(BF16))
    q_in = (q * jnp.exp(cum)).astype(BF16)
    yield
    st = state[0]
    o_inter = _dot_nt(q_in, st.astype(BF16))
    out.append(jnp.concatenate([intra[hh] + o_inter[:, hh * B_DV:(hh + 1) * B_DV] for hh in range(2)], axis=1))
    row2 = lax.broadcasted_iota(jnp.int32, st.shape, 0)
    lane2 = lax.broadcasted_iota(jnp.int32, st.shape, 1)
    state[0] = jnp.where((row2 < B_DV) == (lane2 < B_DK), jnp.exp(total) * st + upd, 0.0)


def _gla_kernel(gin_f_ref, s0_f_ref, gin_b_ref, s0_b_ref, tri_ref, lv_ref,
                o_f_ref, o_b_ref, sfin_f_ref, sfin_b_ref, st_ref, *, first_last):
    n = pl.program_id(0)
    c = GLA_CHUNK
    bqk = B_HEADS * B_DK
    bv = B_HEADS * B_DV
    pairs = B_HEADS // 2
    row2 = lax.broadcasted_iota(jnp.int32, (2 * B_DV, LANES), 0)
    lane2 = lax.broadcasted_iota(jnp.int32, (2 * B_DV, LANES), 1)
    diag_blocks = (row2 < B_DV) == (lane2 < B_DK)
    dirs = ((gin_f_ref, s0_f_ref, o_f_ref, sfin_f_ref), (gin_b_ref, s0_b_ref, o_b_ref, sfin_b_ref))
    for d, (_, s0_ref, _, _) in enumerate(dirs):
        is_first, _, is_ctx = first_last(d, n)

        @pl.when(is_first)
        def _():
            keep = diag_blocks & jnp.logical_not(is_ctx)
            for pr in range(pairs):
                s0t = jnp.concatenate([s0_ref[2 * pr], s0_ref[2 * pr + 1]], axis=0).T
                st_ref[d, pr] = jnp.where(keep, jnp.concatenate([s0t, s0t], axis=0), 0.0)

    n_sub = gin_f_ref.shape[0] // c
    chains, states = [], {}
    for d, (gin_ref, _, o_ref, _) in enumerate(dirs):
        la0 = 2 * bqk + bv + d * bqk
        for pr in range(pairs):
            states[d, pr] = [st_ref[d, pr]]
            for ci in (reversed(range(n_sub)) if d == 1 else range(n_sub)):
                rows = slice(ci * c, (ci + 1) * c)
                out = []
                gen = _gla_pair(gin_ref[rows, pr * LANES:(pr + 1) * LANES],
                                gin_ref[rows, bqk + pr * LANES:bqk + (pr + 1) * LANES],
                                gin_ref[rows, 2 * bqk + pr * 2 * B_DV:2 * bqk + (pr + 1) * 2 * B_DV],
                                gin_ref[rows, la0 + pr * LANES:la0 + (pr + 1) * LANES],
                                tri_ref[d], lv_ref[d], states[d, pr], out, reverse=(d == 1))
                chains.append((gen, out, o_ref, rows, pr))
    running = True
    while running:
        running = False
        for gen, _, _, _, _ in chains:
            for _ in gen:
                running = True
                break
    for _, out, o_ref, rows, pr in chains:
        o_ref[rows, pr * 2 * B_DV:(pr + 1) * 2 * B_DV] = out[0]
    for (d, pr), state in states.items():
        st_ref[d, pr] = state[0]

    for d, (_, _, _, sfin_ref) in enumerate(dirs):
        _, is_last, is_ctx = first_last(d, n)

        @pl.when(is_last & is_ctx)
        def _():
            for pr in range(pairs):
                st = st_ref[d, pr]
                for hh in range(2):
                    blk = st[hh * B_DV:(hh + 1) * B_DV, :].T
                    sfin_ref[2 * pr + hh] = blk[hh * B_DK:(hh + 1) * B_DK, :]


def _gla(gin, state, j, consts, geom):
    n = gin.shape[0]
    c = min(GLA_STEP, geom.ctx_len)
    assert c % GLA_CHUNK == 0
    tri, lv2 = consts
    n_chunks = n // c
    ctx_chunks = geom.n_ctx // c
    cpc = geom.ctx_len // c
    cpl = geom.lat_len // c
    n_seq = geom.n_ctx_seq + geom.n_lat_seq
    bv = B_HEADS * B_DV

    def bwd(nn):
        return n_chunks - 1 - nn

    def seq_of(g):
        return jnp.where(g < ctx_chunks, g // cpc, geom.n_ctx_seq + (g - ctx_chunks) // cpl)

    def first_last(d, nn):
        g = bwd(nn) if d == 1 else nn
        pos = jnp.where(g < ctx_chunks, g % cpc, (g - ctx_chunks) % cpl)
        per = jnp.where(g < ctx_chunks, cpc, cpl)
        at_start, at_end = pos == 0, pos == per - 1
        return ((at_end, at_start) if d == 1 else (at_start, at_end)) + (g < ctx_chunks,)

    def lat_seq(g):
        return jnp.clip(seq_of(g) - geom.n_ctx_seq, 0, geom.n_lat_seq - 1)

    def ctx_seq(g):
        return jnp.minimum(seq_of(g), geom.n_ctx_seq - 1)

    def specs(idx, d):
        return [pl.BlockSpec((c, gin.shape[1]), lambda nn: (idx(nn), 0)),
                pl.BlockSpec((None, None, None, B_HEADS, B_DK, B_DV), lambda nn: (lat_seq(idx(nn)), j, d, 0, 0, 0))]

    fwd = lambda nn: nn
    sfin_shape = jax.ShapeDtypeStruct((geom.n_ctx_seq, B_HEADS, B_DK, B_DV), F32)
    return pl.pallas_call(
        functools.partial(_gla_kernel, first_last=first_last),
        grid=(n_chunks,),
        in_specs=specs(fwd, 0) + specs(bwd, 1) + [_resident(tri.shape), _resident(lv2.shape)],
        out_specs=[pl.BlockSpec((c, bv), lambda nn: (nn, 0)),
                   pl.BlockSpec((c, bv), lambda nn: (bwd(nn), 0)),
                   pl.BlockSpec((None, B_HEADS, B_DK, B_DV), lambda nn: (ctx_seq(nn), 0, 0, 0)),
                   pl.BlockSpec((None, B_HEADS, B_DK, B_DV), lambda nn: (ctx_seq(bwd(nn)), 0, 0, 0))],
        out_shape=[jax.ShapeDtypeStruct((n, bv), F32), jax.ShapeDtypeStruct((n, bv), F32), sfin_shape, sfin_shape],
        scratch_shapes=[pltpu.VMEM((2, B_HEADS // 2, 2 * B_DV, LANES), F32)],
        compiler_params=_cparams(1),
        name="gla",
    )(gin, state, gin, state, tri, lv2)


def _odd_in_kernel(xp_ref, x_ref, xn_ref, mod_ref, g_ref, w_ref, cw_ref, cb_ref, gy_ref, xr_ref, s_ref, *, geom):
    i = pl.program_id(0)
    t = geom.tile
    first, last = geom.seq_edges(i)
    h_ext = _modulated_ext(xp_ref, x_ref, xn_ref, g_ref, mod_ref[0:1, :], mod_ref[1:2, :], first, last)
    yx_ext = _dot(h_ext, w_ref[...])
    d_rnn = gy_ref.shape[1]
    gy_ref[...] = jax.nn.gelu(yx_ext[HALO:HALO + t, 0:d_rnn])
    left = CONV_W // 2
    for sl in range(d_rnn // LANES):
        cols = slice(sl * LANES, (sl + 1) * LANES)
        s_ref[sl] = yx_ext[:, d_rnn + sl * LANES:d_rnn + (sl + 1) * LANES]
        xr = cb_ref[:, cols]
        for tap in range(CONV_W):
            xr = xr + s_ref[sl, pl.ds(HALO - left + tap, t), :] * cw_ref[tap:tap + 1, cols]
        xr_ref[:, cols] = xr


def _odd_in(x, mod, layer, g, w_in, cw, cb, geom):
    n, d = x.shape
    t = geom.tile
    d_rnn = w_in.shape[1] // 2
    return pl.pallas_call(
        functools.partial(_odd_in_kernel, geom=geom),
        grid=(geom.tiles,),
        in_specs=_halo_specs(geom, d) + [_mod_spec(geom, layer, d)] +
                 [_resident(a.shape) for a in (g, w_in, cw, cb)],
        out_specs=[_row_spec(geom, d_rnn), _row_spec(geom, d_rnn)],
        out_shape=[jax.ShapeDtypeStruct((n, d_rnn), F32), jax.ShapeDtypeStruct((n, d_rnn), F32)],
        scratch_shapes=[pltpu.VMEM((d_rnn // LANES, t + 2 * HALO, LANES), F32)],
        compiler_params=_cparams(1),
        name="odd_in",
    )(x, x, x, mod, g, w_in, cw, cb)


def _scan_slab(a, u, carry, reverse):
    rows = a.shape[0]
    groups = rows // SUBLANES
    a3 = a.reshape(groups, SUBLANES, LANES)
    u3 = u.reshape(groups, SUBLANES, LANES)
    sub = lax.broadcasted_iota(jnp.int32, a3.shape, 1)
    k = 1
    while k < SUBLANES:
        shift = SUBLANES - k if reverse else k
        ok = sub < SUBLANES - k if reverse else sub >= k
        a_s = jnp.where(ok, pltpu.roll(a3, shift, 1), 1.0)
        u_s = jnp.where(ok, pltpu.roll(u3, shift, 1), 0.0)
        u3 = a3 * u_s + u3
        a3 = a3 * a_s
        k *= 2
    hs = [None] * groups
    edge = 0 if reverse else SUBLANES - 1
    for gi in (reversed(range(groups)) if reverse else range(groups)):
        h = a3[gi] * carry + u3[gi]
        hs[gi] = h
        carry = h[edge:edge + 1, :]
    return jnp.concatenate(hs, axis=0), carry


def _odd_scan_kernel(xr_ref, wg_ref, bg_ref, lam_ref, h0_ref, h_ref, carry_ref, *, first_of, reverse):
    n = pl.program_id(0)

    is_first, is_ctx = first_of(n)

    @pl.when(is_first)
    def _():
        carry_ref[...] = jnp.where(is_ctx, 0.0, h0_ref[...])

    xr = xr_ref[...]
    xr_b = xr.astype(BF16)
    log_sig_lam = jax.nn.log_sigmoid(lam_ref[...])
    n_grp, grp = wg_ref.shape[0], wg_ref.shape[1]
    for gi in range(n_grp):
        cols = slice(gi * grp, (gi + 1) * grp)
        ri = _dot(xr_b[:, cols], wg_ref[gi]) + bg_ref[gi]
        r_gate = jax.nn.sigmoid(ri[:, :grp])
        i_gate = jax.nn.sigmoid(ri[:, grp:])
        log_a = RG_C * r_gate * log_sig_lam[:, cols]
        a = jnp.exp(log_a)
        u = jnp.sqrt(-jnp.tanh(log_a) * (a * a + 1.0)) * (i_gate * xr[:, cols])
        for sl in range(grp // LANES):
            lanes = slice(gi * grp + sl * LANES, gi * grp + (sl + 1) * LANES)
            h, carry = _scan_slab(a[:, sl * LANES:(sl + 1) * LANES], u[:, sl * LANES:(sl + 1) * LANES],
                                  carry_ref[:, lanes], reverse)
            h_ref[:, lanes] = h
            carry_ref[:, lanes] = carry


def _odd_scan(xr, wg, bg, lam, state, j, geom, direction):
    n, d = xr.shape
    ts = SCAN_TILE
    n_tiles = n // ts
    ctx_tiles = geom.n_ctx // ts
    tpc = geom.ctx_len // ts
    tpl = geom.lat_len // ts
    reverse = direction == 1

    def tile(nn):
        return n_tiles - 1 - nn if reverse else nn

    def seq_of(g):
        return jnp.where(g < ctx_tiles, g // tpc, geom.n_ctx_seq + (g - ctx_tiles) // tpl)

    def first_of(nn):
        g = tile(nn)
        pos = jnp.where(g < ctx_tiles, g % tpc, (g - ctx_tiles) % tpl)
        per = jnp.where(g < ctx_tiles, tpc, tpl)
        return (pos == per - 1 if reverse else pos == 0), g < ctx_tiles

    def lat_seq(g):
        return jnp.clip(seq_of(g) - geom.n_ctx_seq, 0, geom.n_lat_seq - 1)

    blk = pl.BlockSpec((ts, d), lambda nn: (tile(nn), 0))
    return pl.pallas_call(
        functools.partial(_odd_scan_kernel, first_of=first_of, reverse=reverse),
        grid=(n_tiles,),
        in_specs=[blk, _resident(wg.shape), _resident(bg.shape), _resident(lam.shape),
                  pl.BlockSpec((None, None, None, 1, d), lambda nn: (lat_seq(tile(nn)), j, direction, 0, 0))],
        out_specs=blk,
        out_shape=jax.ShapeDtypeStruct((n, d), F32),
        scratch_shapes=[pltpu.VMEM((1, d), F32)],
        compiler_params=_cparams(1),
        name="odd_scan",
    )(xr, wg, bg, lam, state)


def _modulated_ext(xp_ref, x_ref, xn_ref, g_ref, shift, scale, first, last):
    g = g_ref[...]
    h_p = _rms(xp_ref[...], g) * (1.0 + scale) + shift
    h_c = _rms(x_ref[...], g) * (1.0 + scale) + shift
    h_n = _rms(xn_ref[...], g) * (1.0 + scale) + shift
    h_p = jnp.where(first, 0.0, h_p)
    h_n = jnp.where(last, 0.0, h_n)
    return jnp.concatenate([h_p, h_c, h_n], axis=0).astype(BF16)


def _ext(refs3):
    return jnp.concatenate([r[...] for r in refs3], axis=0)


def _mix_ffn_kernel(*refs, geom, final, kind):
    n_mix = 3 if kind == "odd" else 4
    x3 = refs[0:3]
    mix3 = [refs[3 + 3 * j:6 + 3 * j] for j in range(n_mix)]
    pos = 3 + 3 * n_mix
    mod_ref, g_ref = refs[pos:pos + 2]
    pos += 2
    if kind == "even":
        gn_ref = refs[pos]
        pos += 1
    wo_ref, wu_ref, cw_ref, cb_ref, wd_ref, fn_ref = refs[pos:pos + 6]
    out_refs = refs[pos + 6:-1]
    s_ref = refs[-1]

    i = pl.program_id(0)
    t = geom.tile
    first, last = geom.seq_edges(i)
    if kind == "odd":
        z = _ext(mix3[0]) * (_ext(mix3[1]) + _ext(mix3[2]))
    else:
        ob = _ext(mix3[1]) + _ext(mix3[2])
        gb = _ext(mix3[3])
        gn = gn_ref[...]
        z = jnp.concatenate(
            [_ext(mix3[0])] + [_rms(ob[:, hh * B_DV:(hh + 1) * B_DV], gn) * _silu(gb[:, hh * B_DV:(hh + 1) * B_DV])
                               for hh in range(B_HEADS)], axis=1)
    x_new = _ext(x3) + mod_ref[2:3, :] * _dot(z.astype(BF16), wo_ref[...])
    h = _rms(x_new, g_ref[...]) * (1.0 + mod_ref[4:5, :]) + mod_ref[3:4, :]
    row = lax.broadcasted_iota(jnp.int32, (t + 2 * HALO, 1), 0)
    beyond = (first & (row < HALO)) | (last & (row >= HALO + t))
    h_ext = jnp.where(beyond, 0.0, h).astype(BF16)
    n_chunks = wd_ref.shape[0]
    n_slab = wu_ref.shape[3] // LANES

    def project(c):
        for part in range(2):
            u_ext = _dot(h_ext, wu_ref[part, c])
            for sl in range(n_slab):
                s_ref[c % 2, part * n_slab + sl] = u_ext[:, sl * LANES:(sl + 1) * LANES]

    acc = None
    project(0)
    for c in range(n_chunks):
        if c + 1 < n_chunks:
            project(c + 1)
        acts = []
        for sl in range(n_slab):
            cols = slice(sl * LANES, (sl + 1) * LANES)
            halves = []
            for part in range(2):
                slab = s_ref.at[c % 2, part * n_slab + sl]
                cw = cw_ref[part, c]
                halves.append(slab[pl.ds(HALO - 1, t), :] * cw[0:1, cols]
                              + slab[pl.ds(HALO, t), :] * cw[1:2, cols]
                              + slab[pl.ds(HALO + 1, t), :] * cw[2:3, cols]
                              + cb_ref[part, c][:, cols])
            acts.append((_silu(halves[0]) * halves[1]).astype(BF16))
        part_out = _dot(jnp.concatenate(acts, axis=1), wd_ref[c])
        acc = part_out if acc is None else acc + part_out
    y = x_new[HALO:HALO + t] + mod_ref[5:6, :] * acc
    if not final:
        out_refs[0][...] = y
    else:
        y = _rms(y, fn_ref[...])

        @pl.when(i < geom.ctx_tiles)
        def _():
            out_refs[0][...] = y

        @pl.when(i >= geom.ctx_tiles)
        def _():
            out_refs[1][...] = y


def _mix_ffn(x, mod, layer, g, mix_inputs, mix_params, wu, cw, cb, wd, fn, geom, final, kind):
    n, d = x.shape
    t = geom.tile
    n_slab = wu.shape[3] // LANES
    tiled = [x] + list(mix_inputs)
    in_specs = []
    operands = []
    for a in tiled:
        in_specs += _halo_specs(geom, a.shape[1])
        operands += [a, a, a]
    residents = [g] + list(mix_params) + [wu, cw, cb, wd, fn]
    in_specs += [_mod_spec(geom, layer, d)] + [_resident(a.shape) for a in residents]
    operands += [mod] + residents
    if final:
        ct = geom.ctx_tiles
        out_specs = [pl.BlockSpec((t, d), lambda i: (jnp.minimum(i, ct - 1), 0)),
                     pl.BlockSpec((t, d), lambda i: (jnp.maximum(i - ct, 0), 0))]
        out_shape = [jax.ShapeDtypeStruct((geom.n_ctx, d), F32), jax.ShapeDtypeStruct((geom.n_lat, d), F32)]
    else:
        out_specs = [_row_spec(geom, d)]
        out_shape = [jax.ShapeDtypeStruct((n, d), F32)]
    return pl.pallas_call(
        functools.partial(_mix_ffn_kernel, geom=geom, final=final, kind=kind),
        grid=(geom.tiles,),
        in_specs=in_specs,
        out_specs=out_specs,
        out_shape=out_shape,
        scratch_shapes=[pltpu.VMEM((2, 2 * n_slab, t + 2 * HALO, LANES), F32)],
        compiler_params=_cparams(1),
        name="mix_ffn_" + kind,
    )(*operands)


def _rope_tables(t_len):
    rows = t_len // GRID_W
    row = jnp.repeat(jnp.arange(rows, dtype=F32), GRID_W)
    col = jnp.tile(jnp.arange(GRID_W, dtype=F32), rows)
    n_freq = HEAD_DIM // 4
    inv_freq = jnp.power(ROPE_BASE, -jnp.arange(n_freq, dtype=F32) / n_freq)
    ang_r = row[:, None] * inv_freq
    ang_c = col[:, None] * inv_freq
    cos_h = jnp.concatenate([jnp.cos(ang_r)] * 2 + [jnp.cos(ang_c)] * 2, axis=1)
    sin_h = jnp.concatenate([-jnp.sin(ang_r), jnp.sin(ang_r), -jnp.sin(ang_c), jnp.sin(ang_c)], axis=1)
    reps = LANES // HEAD_DIM
    return jnp.tile(cos_h, (1, reps)), jnp.tile(sin_h, (1, reps))


def _block_diag_groups(w, grp):
    nb, bw, _ = w.shape
    per = grp // bw
    w = w.reshape(nb // per, per, bw, bw)
    eye = jnp.eye(per, dtype=w.dtype)
    return jnp.einsum('gpij,pq->gpiqj', w, eye).reshape(nb // per, grp, grp)


def kernel(x_prompt, x_sample, cache_attn_k, cache_attn_v, state_gla, state_rglru, c, c_ctx,
           norm_mix, norm_ffn, w_ada, b_ada,
           ev_w_in, ev_sink, ev_w_gate_f, ev_b_gate_f, ev_w_gate_b, ev_b_gate_b, ev_gla_norm, ev_w_out,
           od_w_in, od_conv_w, od_conv_b, od_w_a, od_b_a, od_w_i, od_b_i, od_lambda, od_w_out,
           ffn_w_up, ffn_conv_w, ffn_conv_b, ffn_w_down, final_norm):
    batch, seq, d = x_prompt.shape
    dec_batch, dec_seq, _ = x_sample.shape
    depth = w_ada.shape[0]
    n_even = ev_w_in.shape[0]
    geom = _Geom(batch, seq, dec_batch, dec_seq, min(ROW_TILE, dec_seq))
    geom_c = _Geom(batch, seq, dec_batch, dec_seq, min(CONV_TILE, seq))
    n_seq = batch + dec_batch

    cond = jnp.concatenate([c_ctx[None, :], c], axis=0)
    cond = jnp.pad(cond, ((0, SUBLANES - cond.shape[0] % SUBLANES), (0, 0)))
    mod = _ada(cond, w_ada, b_ada)

    x = jnp.concatenate([x_prompt.reshape(batch * seq, d), x_sample.reshape(dec_batch * dec_seq, d)], axis=0)
    cos_t, sin_t = _rope_tables(dec_seq)
    gla_consts = _gla_constants(GLA_CHUNK)
    akv = A_KV_HEADS * HEAD_DIM
    bqk = B_HEADS * B_DK
    d_ff = ffn_w_down.shape[1]
    n_ff = d_ff // FF_CHUNK

    new_k, new_v, new_gla, new_rg = [], [], [], []
    for layer in range(depth):
        j = layer // 2
        g_mix = norm_mix[layer][None, :]
        if layer % 2 == 0:
            w_in = ev_w_in[j]
            main = w_in.shape[1] - 2 * GATE_RANK
            w_in_p = jnp.pad(w_in, ((0, 0), (0, LANES - 2 * GATE_RANK))).astype(BF16)
            wg = jnp.zeros((LANES, 2 * bqk), F32)
            wg = wg.at[0:GATE_RANK, 0:bqk].set(ev_w_gate_f[j])
            wg = wg.at[GATE_RANK:2 * GATE_RANK, bqk:2 * bqk].set(ev_w_gate_b[j]).astype(BF16)
            bg = jnp.concatenate([ev_b_gate_f[j], ev_b_gate_b[j]])[None, :]
            assert main % LANES == 0
            qa, ka, va, gb, gin = _even_in(x, mod, layer, g_mix, w_in_p, wg, bg, cos_t, sin_t, geom)
            ck = cache_attn_k[:, j].reshape(dec_batch, -1, akv)
            cv = cache_attn_v[:, j].reshape(dec_batch, -1, akv)
            oa = _attention(qa, ka, va, ev_sink[j], ck, cv, geom)
            o_f, o_b, s_f, s_b = _gla(gin, state_gla, j, gla_consts, geom)
            kind, mix_inputs = "even", [oa, o_f, o_b, gb]
            mix_params = [ev_gla_norm[j][None, :], ev_w_out[j].astype(BF16)]
            new_k.append(ka[:geom.n_ctx].reshape(batch, seq, A_KV_HEADS, HEAD_DIM))
            new_v.append(va[:geom.n_ctx].reshape(batch, seq, A_KV_HEADS, HEAD_DIM))
            new_gla.append(jnp.stack([s_f, s_b], axis=1))
        else:
            w_in = od_w_in[j].astype(BF16)
            d_rnn = w_in.shape[1] // 2
            grp = 2 * LANES
            wa = jnp.stack([_block_diag_groups(od_w_a[j, dr], grp) for dr in range(2)])
            wi = jnp.stack([_block_diag_groups(od_w_i[j, dr], grp) for dr in range(2)])
            wgate = jnp.concatenate([wa, wi], axis=-1).astype(BF16)
            ba = od_b_a[j].reshape(2, d_rnn // grp, 1, grp)
            bi = od_b_i[j].reshape(2, d_rnn // grp, 1, grp)
            bgate = jnp.concatenate([ba, bi], axis=-1)
            gy, xr = _odd_in(x, mod, layer, g_mix, w_in, od_conv_w[j], od_conv_b[j][None, :], geom_c)
            h0 = state_rglru.reshape(dec_batch, -1, 2, 1, d_rnn)
            h_f, h_b = [_odd_scan(xr, wgate[dr], bgate[dr], od_lambda[j, dr][None, :], h0, j, geom, dr)
                        for dr in range(2)]
            kind, mix_inputs, mix_params = "odd", [gy, h_f, h_b], [od_w_out[j].astype(BF16)]
            new_rg.append(jnp.stack([h_f[:geom.n_ctx].reshape(batch, seq, d_rnn)[:, -1],
                                     h_b[:geom.n_ctx].reshape(batch, seq, d_rnn)[:, 0]], axis=1))
        wu = ffn_w_up[layer].astype(BF16).reshape(d, 2, n_ff, FF_CHUNK).transpose(1, 2, 0, 3)
        cw = ffn_conv_w[layer].reshape(FFN_CONV_W, 2, n_ff, FF_CHUNK).transpose(1, 2, 0, 3)
        cb = ffn_conv_b[layer].reshape(2, n_ff, 1, FF_CHUNK)
        wd = ffn_w_down[layer].astype(BF16).reshape(n_ff, FF_CHUNK, d)
        outs = _mix_ffn(x, mod, layer, norm_ffn[layer][None, :], mix_inputs, mix_params, wu, cw, cb, wd,
                        final_norm[None, :], geom_c, final=(layer == depth - 1), kind=kind)
        x = outs[0]

    y_prompt = outs[0].reshape(batch, seq, d)
    y_sample = outs[1].reshape(dec_batch, dec_seq, d)
    return (y_prompt, y_sample, jnp.stack(new_k, axis=1), jnp.stack(new_v, axis=1),
            jnp.stack(new_gla, axis=1), jnp.stack(new_rg, axis=1))
```

```python
import functools

import jax
import jax.numpy as jnp
import numpy as np
from jax import lax
from jax.experimental import pallas as pl
from jax.experimental.pallas import tpu as pltpu

F32 = jnp.float32
BF16 = jnp.bfloat16

EPS = 1e-6
NEG_INF = -1e30
GRID_W = 64
A_HEADS = 8
A_KV_HEADS = 2
HEAD_DIM = 64
WINDOW = 128
ATT_BLOCK = 128
ROPE_BASE = 10000.0
B_HEADS = 4
B_DK = 64
B_DV = 128
GATE_RANK = 16
GATE_NORMALIZER = 16.0
RG_BLOCKS = 16
RG_C = 8.0
CONV_W = 4
FFN_CONV_W = 3

LANES = 128
SUBLANES = 8
HALO = SUBLANES
ROW_TILE = 512
CONV_TILE = 256
GLA_CHUNK = 128
GLA_STEP = 256
SCAN_TILE = 256
SCAN_PITCH = SCAN_TILE // SUBLANES + 4
FF_CHUNK = 256
VMEM_LIMIT = 56 * 1024 * 1024


def _cparams(n_axes):
    return pltpu.CompilerParams(dimension_semantics=("arbitrary",) * n_axes,
                                vmem_limit_bytes=VMEM_LIMIT)


def _resident(shape):
    nd = len(shape)
    return pl.BlockSpec(shape, lambda *_: (0,) * nd, pipeline_mode=pl.Buffered(1))


def _silu(x):
    return x * jax.nn.sigmoid(x)


def _rms(x, g):
    return x * lax.rsqrt(jnp.mean(x * x, axis=-1, keepdims=True) + EPS) * g


def _dot(a, b):
    return jnp.dot(a, b, preferred_element_type=F32)


def _dot_nt(a, b):
    return lax.dot_general(a, b, (((1,), (1,)), ((), ())), preferred_element_type=F32)


def _dot_tn(a, b):
    return lax.dot_general(a, b, (((0,), (0,)), ((), ())), preferred_element_type=F32)


def _ada_kernel(c_ref, w_ref, b_ref, o_ref):
    s = _silu(c_ref[...])
    o_ref[...] = _dot(s.astype(BF16), w_ref[...].astype(BF16)) + b_ref[...]


def _ada(cond, w_ada, b_ada):
    depth, d, d6 = w_ada.shape
    nb = 4
    cb = d6 // nb
    rows = cond.shape[0]
    out = pl.pallas_call(
        _ada_kernel,
        grid=(depth, nb),
        in_specs=[pl.BlockSpec((rows, d), lambda l, j: (0, 0)),
                  pl.BlockSpec((None, d, cb), lambda l, j: (l, 0, j)),
                  pl.BlockSpec((None, 1, cb), lambda l, j: (l, 0, j))],
        out_specs=pl.BlockSpec((None, rows, cb), lambda l, j: (l, 0, j)),
        out_shape=jax.ShapeDtypeStruct((depth, rows, d6), F32),
        compiler_params=_cparams(2),
        name="ada",
    )(cond, w_ada, b_ada.reshape(depth, 1, d6))
    return out.reshape(depth, rows, 6, d)


class _Geom:
    def __init__(self, n_ctx_seq, ctx_len, n_lat_seq, lat_len, tile):
        assert ctx_len % tile == 0 or tile % ctx_len == 0
        assert lat_len % tile == 0
        self.n_ctx_seq, self.ctx_len, self.n_lat_seq, self.lat_len = n_ctx_seq, ctx_len, n_lat_seq, lat_len
        self.tile = tile
        self.n_ctx = n_ctx_seq * ctx_len
        self.n_lat = n_lat_seq * lat_len
        self.n = self.n_ctx + self.n_lat
        assert self.n_ctx % tile == 0
        self.ctx_tiles = self.n_ctx // tile
        self.lat_tiles_per_seq = lat_len // tile
        self.tiles = self.n // tile

    def mod_row(self, i):
        return jnp.where(i < self.ctx_tiles, 0, 1 + (i - self.ctx_tiles) // self.lat_tiles_per_seq)

    def lat_tile(self, i):
        return jnp.where(i < self.ctx_tiles, 0, (i - self.ctx_tiles) % self.lat_tiles_per_seq)

    def seq_edges(self, i):
        assert self.ctx_len % self.tile == 0
        is_ctx = i < self.ctx_tiles
        seq_len = jnp.where(is_ctx, self.ctx_len, self.lat_len)
        start = (i * self.tile - jnp.where(is_ctx, 0, self.n_ctx)) % seq_len
        return start == 0, start + self.tile == seq_len


def _halo_specs(geom, d):
    t = geom.tile
    per = t // HALO
    last = geom.n // HALO - 1
    return [pl.BlockSpec((HALO, d), lambda i: (jnp.maximum(i * per - 1, 0), 0)),
            pl.BlockSpec((t, d), lambda i: (i, 0)),
            pl.BlockSpec((HALO, d), lambda i: (jnp.minimum((i + 1) * per, last), 0))]


def _mod_spec(geom, layer, d):
    return pl.BlockSpec((None, None, 6, d), lambda i: (layer, geom.mod_row(i), 0, 0))


def _row_spec(geom, width, col=0):
    return pl.BlockSpec((geom.tile, width), lambda i: (i, col))


def _rope_apply(t, cos, sin):
    lane = lax.broadcasted_iota(jnp.int32, t.shape, 1)
    swapped = jnp.where((lane & 16) == 0, pltpu.roll(t, LANES - 16, 1), pltpu.roll(t, 16, 1))
    return t * cos + swapped * sin


def _even_in_kernel(*refs, geom, split_x):
    i = pl.program_id(0)
    is_lat = i >= geom.ctx_tiles
    if split_x:
        xc_ref, xl_ref = refs[0:2]
        xcat_ref = refs[-1]
        refs = refs[2:-1]
        x = jnp.where(is_lat, xl_ref[...], xc_ref[...])
        xcat_ref[...] = x
    else:
        x = refs[0][...]
        refs = refs[1:]
    mod_ref, g_ref, w_ref, wg_ref, bg_ref, cos_ref, sin_ref, qa_ref, ka_ref, va_ref, gb_ref, gin_ref = refs
    h = _rms(x, g_ref[...]) * (1.0 + mod_ref[1:2, :]) + mod_ref[0:1, :]
    p = _dot(h.astype(BF16), w_ref[...])
    aq = A_HEADS * HEAD_DIM
    akv = A_KV_HEADS * HEAD_DIM
    bqk = B_HEADS * B_DK
    bv = B_HEADS * B_DV
    o = 0
    q_a = p[:, o:o + aq] * (HEAD_DIM ** -0.5); o += aq
    k_a = p[:, o:o + akv]; o += akv
    va_ref[...] = p[:, o:o + akv]; o += akv
    gin_ref[:, 0:bqk] = p[:, o:o + bqk] * (B_DK ** -0.5); o += bqk
    gin_ref[:, bqk:2 * bqk] = p[:, o:o + bqk]; o += bqk
    gin_ref[:, 2 * bqk:2 * bqk + bv] = p[:, o:o + bv]; o += bv
    gb_ref[...] = p[:, o:o + bv]; o += bv
    r = p[:, o:o + LANES]
    z = _dot(r.astype(BF16), wg_ref[...]) + bg_ref[...]
    gin_ref[:, 2 * bqk + bv:4 * bqk + bv] = jax.nn.log_sigmoid(z) * (1.0 / GATE_NORMALIZER)

    cos, sin = cos_ref[...], sin_ref[...]
    for j in range(aq // LANES):
        blk = q_a[:, j * LANES:(j + 1) * LANES]
        qa_ref[:, j * LANES:(j + 1) * LANES] = jnp.where(is_lat, _rope_apply(blk, cos, sin), blk).astype(BF16)
    for j in range(akv // LANES):
        blk = k_a[:, j * LANES:(j + 1) * LANES]
        ka_ref[:, j * LANES:(j + 1) * LANES] = jnp.where(is_lat, _rope_apply(blk, cos, sin), blk)


def _even_in(xs, mod, layer, g, w_in_p, wg, bg, cos_t, sin_t, geom):
    d = xs[0].shape[1]
    n = geom.n
    t = geom.tile
    aq = A_HEADS * HEAD_DIM
    akv = A_KV_HEADS * HEAD_DIM
    bqk = B_HEADS * B_DK
    bv = B_HEADS * B_DV
    split_x = len(xs) == 2
    ct = geom.ctx_tiles
    if split_x:
        x_specs = [pl.BlockSpec((t, d), lambda i: (jnp.minimum(i, ct - 1), 0)),
                   pl.BlockSpec((t, d), lambda i: (jnp.maximum(i - ct, 0), 0))]
    else:
        x_specs = [_row_spec(geom, d)]
    outs = [(aq, BF16), (akv, F32), (akv, F32), (bv, F32), (4 * bqk + bv, F32)] + ([(d, F32)] if split_x else [])
    return pl.pallas_call(
        functools.partial(_even_in_kernel, geom=geom, split_x=split_x),
        grid=(geom.tiles,),
        in_specs=x_specs + [_mod_spec(geom, layer, d), _resident(g.shape),
                            _resident(w_in_p.shape), _resident(wg.shape), _resident(bg.shape),
                            pl.BlockSpec((t, LANES), lambda i: (geom.lat_tile(i), 0)),
                            pl.BlockSpec((t, LANES), lambda i: (geom.lat_tile(i), 0))],
        out_specs=[_row_spec(geom, w) for w, _ in outs],
        out_shape=[jax.ShapeDtypeStruct((n, w), dt) for w, dt in outs],
        compiler_params=_cparams(1),
        name="even_in",
    )(*xs, mod, g, w_in_p, wg, bg, cos_t, sin_t)


def _attend(q, k, v, mask_fn, sink_ref, o_ref):
    nq, nk = q.shape[0], k.shape[0]
    lane = lax.broadcasted_iota(jnp.int32, k.shape, 1)
    lo = lane < HEAD_DIM
    k_sw = pltpu.roll(k, HEAD_DIM, 1)
    v_sw = pltpu.roll(v, HEAD_DIM, 1)
    kk = [[jnp.where(lo, k, 0.0).astype(BF16), jnp.where(lo, 0.0, k_sw).astype(BF16)],
          [jnp.where(lo, k_sw, 0.0).astype(BF16), jnp.where(lo, 0.0, k).astype(BF16)]]
    vv = [[jnp.where(lo, v, 0.0).astype(BF16), jnp.where(lo, 0.0, v_sw).astype(BF16)],
          [jnp.where(lo, v_sw, 0.0).astype(BF16), jnp.where(lo, 0.0, v).astype(BF16)]]
    group = A_HEADS // A_KV_HEADS
    mask = None
    if mask_fn is not None:
        mask = mask_fn(lax.broadcasted_iota(jnp.int32, (nq, nk), 0), lax.broadcasted_iota(jnp.int32, (nq, nk), 1))
    heads = range(A_HEADS)
    qb = [q[:, pair * LANES:(pair + 1) * LANES].astype(BF16) for pair in range(A_HEADS // 2)]
    s = [_dot_nt(qb[h // 2], kk[h // group][h % 2]) for h in heads]
    if mask is not None:
        s = [jnp.where(mask, sh, NEG_INF) for sh in s]
    m = [jnp.maximum(jnp.max(s[h], axis=1, keepdims=True), sink_ref[h]) for h in heads]
    e = [jnp.exp(s[h] - m[h]) for h in heads]
    inv = [1.0 / (jnp.sum(e[h], axis=1, keepdims=True) + jnp.exp(sink_ref[h] - m[h])) for h in heads]
    o = [_dot((e[h] * inv[h]).astype(BF16), vv[h // group][h % 2]) for h in heads]
    for pair in range(A_HEADS // 2):
        o_ref[:, pair * LANES:(pair + 1) * LANES] = o[2 * pair] + o[2 * pair + 1]


def _attn_kernel(sink_ref, q_ref, kp_ref, kc_ref, kn_ref, vp_ref, vc_ref, vn_ref, kx_ref, vx_ref, o_ref,
                 *, ctx_blocks, lat_blocks):
    g = pl.program_id(0)
    blk = ATT_BLOCK

    @pl.when(g < ctx_blocks)
    def _():
        even = g % 2 == 0
        k = jnp.concatenate([kc_ref[...], jnp.where(even, kn_ref[...], kp_ref[...])], axis=0)
        v = jnp.concatenate([vc_ref[...], jnp.where(even, vn_ref[...], vp_ref[...])], axis=0)
        _attend(q_ref[...], k, v, None, sink_ref, o_ref)

    @pl.when(g >= ctx_blocks)
    def _():
        i = (g - ctx_blocks) % lat_blocks
        k = jnp.concatenate([kp_ref[...], kc_ref[...], kn_ref[...], kx_ref[...]], axis=0)
        v = jnp.concatenate([vp_ref[...], vc_ref[...], vn_ref[...], vx_ref[...]], axis=0)

        def mask_fn(qi, kj):
            in_win = (jnp.abs(kj - blk - qi) <= WINDOW) & (kj < 3 * blk)
            in_win = in_win & ((kj >= blk) | (i > 0)) & ((kj < 2 * blk) | (i < lat_blocks - 1))
            return in_win | (kj >= 3 * blk)

        _attend(q_ref[...], k, v, mask_fn, sink_ref, o_ref)


def _attention(qa, ka, va, sink, cache_k, cache_v, geom):
    aq = qa.shape[1]
    akv = ka.shape[1]
    blk = ATT_BLOCK
    assert geom.ctx_len == 2 * blk and WINDOW <= blk
    n_blocks = geom.n // blk
    ctx_blocks = geom.n_ctx // blk
    lat_blocks = geom.lat_len // blk
    past = cache_k.shape[1]

    def lat_seq(g):
        return jnp.clip((g - ctx_blocks) // lat_blocks, 0, geom.n_lat_seq - 1)

    kv_spec = lambda f: pl.BlockSpec((blk, akv), f)
    prv = lambda g: (jnp.maximum(g - 1, 0), 0)
    cur = lambda g: (g, 0)
    nxt = lambda g: (jnp.minimum(g + 1, n_blocks - 1), 0)
    cache_spec = pl.BlockSpec((None, past, akv), lambda g: (lat_seq(g), 0, 0))
    return pl.pallas_call(
        functools.partial(_attn_kernel, ctx_blocks=ctx_blocks, lat_blocks=lat_blocks),
        grid=(n_blocks,),
        in_specs=[pl.BlockSpec(memory_space=pltpu.SMEM), pl.BlockSpec((blk, aq), cur),
                  kv_spec(prv), kv_spec(cur), kv_spec(nxt),
                  kv_spec(prv), kv_spec(cur), kv_spec(nxt),
                  cache_spec, cache_spec],
        out_specs=pl.BlockSpec((blk, aq), cur),
        out_shape=jax.ShapeDtypeStruct((geom.n, aq), F32),
        compiler_params=_cparams(1),
        name="attn",
    )(sink, qa, ka, ka, ka, va, va, va, cache_k, cache_v)


def _gla_constants(c):
    levels = int(np.log2(c))
    i_idx = np.arange(c)[:, None]
    j_idx = np.arange(c)[None, :]
    lv = np.full((c, c), -1, np.int32)
    lv[np.arange(c), np.arange(c)] = levels
    for lvl in range(levels):
        s = c >> (lvl + 1)
        same_pair = (i_idx // (2 * s)) == (j_idx // (2 * s))
        lv[same_pair & ((i_idx % (2 * s)) >= s) & ((j_idx % (2 * s)) < s)] = lvl
    lv2 = np.stack([np.tile(lv, (1, 2)), np.tile(lv[::-1, ::-1], (1, 2))])
    tri = np.stack([j_idx <= i_idx, j_idx >= i_idx]).astype(np.float32)
    return jnp.asarray(tri, BF16), jnp.asarray(lv2)


def _gla_pair(q, k, v, la, tri, lv2, state, out, reverse):
    c = q.shape[0]
    levels = int(np.log2(c))
    groups = c // SUBLANES
    la_hi = la.astype(BF16)
    rem = la - la_hi.astype(F32)
    la_mid = rem.astype(BF16)
    la_lo = (rem - la_mid.astype(F32)).astype(BF16)
    cum = (_dot(tri, la_hi) + _dot(tri, la_mid)) + _dot(tri, la_lo)
    edge = 0 if reverse else c - 1
    total = cum[edge:edge + 1, :]
    lane = lax.broadcasted_iota(jnp.int32, (c, LANES), 1)
    row = lax.broadcasted_iota(jnp.int32, (c, LANES), 0)
    lo = lane < B_DK

    def heads_on_rows(kt):
        return jnp.concatenate([jnp.where(lo, kt, 0.0), jnp.where(lo, 0.0, kt)], axis=0).astype(BF16)

    def rows_of(x, g):
        return x[g * SUBLANES:(g + 1) * SUBLANES]

    yield
    p = _dot_nt(q.astype(BF16), heads_on_rows(k))
    a = [jnp.where(rows_of(lv2, g) == levels, rows_of(p, g), 0.0) for g in range(groups)]
    for lvl in range(levels):
        yield
        s = c >> (lvl + 1)
        if s >= SUBLANES:
            q_parts, k_parts, q_groups = [], [], []
            zeros = jnp.zeros((s, LANES), F32)
            for b0 in range(0, c, 2 * s):
                first, second = slice(b0, b0 + s), slice(b0 + s, b0 + 2 * s)
                qs, ks, m = (first, second, b0 + s) if reverse else (second, first, b0 + s - 1)
                ref = cum[m:m + 1, :]
                q_parts.append(q[qs] * jnp.exp(cum[qs] - ref))
                k_blk = k[ks] * jnp.exp(ref - cum[ks])
                k_parts += [zeros, k_blk] if reverse else [k_blk, zeros]
                q_groups += list(range(qs.start // SUBLANES, qs.stop // SUBLANES))
            qt = jnp.concatenate(q_parts, axis=0)
            kt = jnp.concatenate(k_parts, axis=0)
        else:
            q_groups = list(range(groups))
            pos = row & (2 * s - 1)
            q_side = (pos < s) if reverse else (pos >= s)
            if s == 1:
                qt = jnp.where(q_side, q * jnp.exp(la), 0.0)
                kt = jnp.where(q_side, 0.0, k)
            else:
                cum3 = cum.reshape(groups, SUBLANES, LANES)
                sub = lax.broadcasted_iota(jnp.int32, cum3.shape, 1)
                ref = None
                for b0 in range(0, SUBLANES, 2 * s):
                    m = b0 + s if reverse else b0 + s - 1
                    r = cum3[:, m:m + 1, :]
                    ref = r if ref is None else jnp.where(sub >= b0, r, ref)
                e = jnp.exp(-jnp.abs(cum3 - ref)).reshape(c, LANES)
                qt = jnp.where(q_side, q * e, 0.0)
                kt = jnp.where(q_side, 0.0, k * e)
        p = _dot_nt(qt.astype(BF16), heads_on_rows(kt))
        for idx, g in enumerate(q_groups):
            a[g] = jnp.where(rows_of(lv2, g) == lvl, rows_of(p, idx), a[g])

    yield
    a_full = jnp.concatenate(a, axis=0).astype(BF16)
    v_b = v.astype(BF16)
    intra = [_dot(a_full[:, hh * c:(hh + 1) * c], v_b[:, hh * B_DV:(hh + 1) * B_DV]) for hh in range(2)]
    upd = _dot_tn(v_b, (k * jnp.exp(total - cum)).astype(BF16))
    q_in = (q * jnp.exp(cum)).astype(BF16)
    yield
    st = state[0]
    o_inter = _dot_nt(q_in, st.astype(BF16))
    out.append(jnp.concatenate([intra[hh] + o_inter[:, hh * B_DV:(hh + 1) * B_DV] for hh in range(2)], axis=1))
    row2 = lax.broadcasted_iota(jnp.int32, st.shape, 0)
    lane2 = lax.broadcasted_iota(jnp.int32, st.shape, 1)
    state[0] = jnp.where((row2 < B_DV) == (lane2 < B_DK), jnp.exp(total) * st + upd, 0.0)


def _gla_kernel(gin_f_ref, s0_f_ref, gin_b_ref, s0_b_ref, tri_ref, lv_ref,
                o_f_ref, o_b_ref, sfin_f_ref, sfin_b_ref, st_ref, *, first_last):
    n = pl.program_id(0)
    c = GLA_CHUNK
    bqk = B_HEADS * B_DK
    bv = B_HEADS * B_DV
    pairs = B_HEADS // 2
    row2 = lax.broadcasted_iota(jnp.int32, (2 * B_DV, LANES), 0)
    lane2 = lax.broadcasted_iota(jnp.int32, (2 * B_DV, LANES), 1)
    diag_blocks = (row2 < B_DV) == (lane2 < B_DK)
    dirs = ((gin_f_ref, s0_f_ref, o_f_ref, sfin_f_ref), (gin_b_ref, s0_b_ref, o_b_ref, sfin_b_ref))
    for d, (_, s0_ref, _, _) in enumerate(dirs):
        is_first, _, is_ctx = first_last(d, n)

        @pl.when(is_first)
        def _():
            keep = diag_blocks & jnp.logical_not(is_ctx)
            for pr in range(pairs):
                s0t = jnp.concatenate([s0_ref[2 * pr], s0_ref[2 * pr + 1]], axis=0).T
                st_ref[d, pr] = jnp.where(keep, jnp.concatenate([s0t, s0t], axis=0), 0.0)

    n_sub = gin_f_ref.shape[0] // c
    chains, states = [], {}
    for d, (gin_ref, _, o_ref, _) in enumerate(dirs):
        la0 = 2 * bqk + bv + d * bqk
        for pr in range(pairs):
            states[d, pr] = [st_ref[d, pr]]
            for ci in (reversed(range(n_sub)) if d == 1 else range(n_sub)):
                rows = slice(ci * c, (ci + 1) * c)
                out = []
                gen = _gla_pair(gin_ref[rows, pr * LANES:(pr + 1) * LANES],
                                gin_ref[rows, bqk + pr * LANES:bqk + (pr + 1) * LANES],
                                gin_ref[rows, 2 * bqk + pr * 2 * B_DV:2 * bqk + (pr + 1) * 2 * B_DV],
                                gin_ref[rows, la0 + pr * LANES:la0 + (pr + 1) * LANES],
                                tri_ref[d], lv_ref[d], states[d, pr], out, reverse=(d == 1))
                chains.append((gen, out, o_ref, rows, pr))
    running = True
    while running:
        running = False
        for gen, _, _, _, _ in chains:
            for _ in gen:
                running = True
                break
    for _, out, o_ref, rows, pr in chains:
        o_ref[rows, pr * 2 * B_DV:(pr + 1) * 2 * B_DV] = out[0]
    for (d, pr), state in states.items():
        st_ref[d, pr] = state[0]

    for d, (_, _, _, sfin_ref) in enumerate(dirs):
        _, is_last, is_ctx = first_last(d, n)

        @pl.when(is_last & is_ctx)
        def _():
            for pr in range(pairs):
                st = st_ref[d, pr]
                for hh in range(2):
                    blk = st[hh * B_DV:(hh + 1) * B_DV, :].T
                    sfin_ref[2 * pr + hh] = blk[hh * B_DK:(hh + 1) * B_DK, :]


def _gla(gin, state, j, consts, geom):
    n = gin.shape[0]
    c = min(GLA_STEP, geom.ctx_len)
    assert c % GLA_CHUNK == 0
    tri, lv2 = consts
    n_chunks = n // c
    ctx_chunks = geom.n_ctx // c
    cpc = geom.ctx_len // c
    cpl = geom.lat_len // c
    bv = B_HEADS * B_DV

    def bwd(nn):
        return n_chunks - 1 - nn

    def seq_of(g):
        return jnp.where(g < ctx_chunks, g // cpc, geom.n_ctx_seq + (g - ctx_chunks) // cpl)

    def first_last(d, nn):
        g = bwd(nn) if d == 1 else nn
        pos = jnp.where(g < ctx_chunks, g % cpc, (g - ctx_chunks) % cpl)
        per = jnp.where(g < ctx_chunks, cpc, cpl)
        at_start, at_end = pos == 0, pos == per - 1
        return ((at_end, at_start) if d == 1 else (at_start, at_end)) + (g < ctx_chunks,)

    def lat_seq(g):
        return jnp.clip(seq_of(g) - geom.n_ctx_seq, 0, geom.n_lat_seq - 1)

    def ctx_seq(g):
        return jnp.minimum(seq_of(g), geom.n_ctx_seq - 1)

    def specs(idx, d):
        return [pl.BlockSpec((c, gin.shape[1]), lambda nn: (idx(nn), 0)),
                pl.BlockSpec((None, None, None, B_HEADS, B_DK, B_DV), lambda nn: (lat_seq(idx(nn)), j, d, 0, 0, 0))]

    fwd = lambda nn: nn
    sfin_shape = jax.ShapeDtypeStruct((geom.n_ctx_seq, B_HEADS, B_DK, B_DV), F32)
    return pl.pallas_call(
        functools.partial(_gla_kernel, first_last=first_last),
        grid=(n_chunks,),
        in_specs=specs(fwd, 0) + specs(bwd, 1) + [_resident(tri.shape), _resident(lv2.shape)],
        out_specs=[pl.BlockSpec((c, bv), lambda nn: (nn, 0)),
                   pl.BlockSpec((c, bv), lambda nn: (bwd(nn), 0)),
                   pl.BlockSpec((None, B_HEADS, B_DK, B_DV), lambda nn: (ctx_seq(nn), 0, 0, 0)),
                   pl.BlockSpec((None, B_HEADS, B_DK, B_DV), lambda nn: (ctx_seq(bwd(nn)), 0, 0, 0))],
        out_shape=[jax.ShapeDtypeStruct((n, bv), F32), jax.ShapeDtypeStruct((n, bv), F32), sfin_shape, sfin_shape],
        scratch_shapes=[pltpu.VMEM((2, B_HEADS // 2, 2 * B_DV, LANES), F32)],
        compiler_params=_cparams(1),
        name="gla",
    )(gin, state, gin, state, tri, lv2)


def _odd_in_kernel(xp_ref, x_ref, xn_ref, mod_ref, g_ref, w_ref, cw_ref, cb_ref, gy_ref, xr_ref, s_ref, *, geom):
    i = pl.program_id(0)
    t = geom.tile
    first, last = geom.seq_edges(i)
    h_ext = _modulated_ext(xp_ref, x_ref, xn_ref, g_ref, mod_ref[0:1, :], mod_ref[1:2, :], first, last)
    yx_ext = _dot(h_ext, w_ref[...])
    d_rnn = gy_ref.shape[1]
    gy_ref[...] = jax.nn.gelu(yx_ext[HALO:HALO + t, 0:d_rnn])
    left = CONV_W // 2
    for sl in range(d_rnn // LANES):
        cols = slice(sl * LANES, (sl + 1) * LANES)
        s_ref[sl] = yx_ext[:, d_rnn + sl * LANES:d_rnn + (sl + 1) * LANES]
        xr = cb_ref[:, cols]
        for tap in range(CONV_W):
            xr = xr + s_ref[sl, pl.ds(HALO - left + tap, t), :] * cw_ref[tap:tap + 1, cols]
        xr_ref[:, cols] = xr


def _odd_in(x, mod, layer, g, w_in, cw, cb, geom):
    n, d = x.shape
    t = geom.tile
    d_rnn = w_in.shape[1] // 2
    return pl.pallas_call(
        functools.partial(_odd_in_kernel, geom=geom),
        grid=(geom.tiles,),
        in_specs=_halo_specs(geom, d) + [_mod_spec(geom, layer, d)] +
                 [_resident(a.shape) for a in (g, w_in, cw, cb)],
        out_specs=[_row_spec(geom, d_rnn), _row_spec(geom, d_rnn)],
        out_shape=[jax.ShapeDtypeStruct((n, d_rnn), F32), jax.ShapeDtypeStruct((n, d_rnn), F32)],
        scratch_shapes=[pltpu.VMEM((d_rnn // LANES, t + 2 * HALO, LANES), F32)],
        compiler_params=_cparams(1),
        name="odd_in",
    )(x, x, x, mod, g, w_in, cw, cb)


def _scan_slab(a, u, carry, reverse):
    rows = a.shape[0]
    groups = rows // SUBLANES
    a3 = a.reshape(groups, SUBLANES, LANES)
    u3 = u.reshape(groups, SUBLANES, LANES)
    sub = lax.broadcasted_iota(jnp.int32, a3.shape, 1)
    k = 1
    while k < SUBLANES:
        shift = SUBLANES - k if reverse else k
        ok = sub < SUBLANES - k if reverse else sub >= k
        a_s = jnp.where(ok, pltpu.roll(a3, shift, 1), 1.0)
        u_s = jnp.where(ok, pltpu.roll(u3, shift, 1), 0.0)
        u3 = a3 * u_s + u3
        a3 = a3 * a_s
        k *= 2
    hs = [None] * groups
    edge = 0 if reverse else SUBLANES - 1
    for gi in (reversed(range(groups)) if reverse else range(groups)):
        h = a3[gi] * carry + u3[gi]
        hs[gi] = h
        carry = h[edge:edge + 1, :]
    return jnp.concatenate(hs, axis=0), carry


def _odd_scan_kernel(xr_ref, wg_ref, bg_ref, lam_ref, h0_ref, h_ref, carry_ref, a_s, u_s, *, first_of, reverse):
    n = pl.program_id(0)
    is_first, is_ctx = first_of(n)

    @pl.when(is_first)
    def _():
        carry_ref[...] = jnp.where(is_ctx, 0.0, h0_ref[...])

    rows = xr_ref.shape[0]
    seg = rows // SUBLANES
    xr = xr_ref[...]
    xr_b = xr.astype(BF16)
    half_scale = (0.5 * RG_C) * jax.nn.log_sigmoid(lam_ref[...])
    n_grp, grp = wg_ref.shape[0], wg_ref.shape[1]
    n_slab = xr_ref.shape[1] // LANES
    for gi in range(n_grp):
        cols = slice(gi * grp, (gi + 1) * grp)
        th = jnp.tanh(_dot(xr_b[:, cols], wg_ref[gi]) + bg_ref[gi])
        log_a = half_scale[:, cols] * th[:, :grp] + half_scale[:, cols]
        i_gate = 0.5 * th[:, grp:] + 0.5
        a = jnp.exp(log_a)
        u = jnp.sqrt(jnp.tanh(log_a) * (-1.0 - a * a)) * (i_gate * xr[:, cols])
        for j in range(grp // LANES):
            sl = gi * (grp // LANES) + j
            for s in range(SUBLANES):
                a_s[sl, pl.ds(s * SCAN_PITCH, seg), :] = a[s * seg:(s + 1) * seg, j * LANES:(j + 1) * LANES]
                u_s[sl, pl.ds(s * SCAN_PITCH, seg), :] = u[s * seg:(s + 1) * seg, j * LANES:(j + 1) * LANES]

    h = [jnp.zeros((SUBLANES, LANES), F32)] * n_slab
    decay = [jnp.ones((SUBLANES, LANES), F32)] * n_slab
    for t in (reversed(range(seg)) if reverse else range(seg)):
        for sl in range(n_slab):
            at = a_s[sl, pl.ds(t, SUBLANES, stride=SCAN_PITCH), :]
            ut = u_s[sl, pl.ds(t, SUBLANES, stride=SCAN_PITCH), :]
            h[sl] = at * h[sl] + ut
            decay[sl] = at * decay[sl]
            u_s[sl, pl.ds(t, SUBLANES, stride=SCAN_PITCH), :] = h[sl]
            a_s[sl, pl.ds(t, SUBLANES, stride=SCAN_PITCH), :] = decay[sl]

    sub = lax.broadcasted_iota(jnp.int32, (SUBLANES, LANES), 0)
    for sl in range(n_slab):
        lanes = slice(sl * LANES, (sl + 1) * LANES)
        carry = carry_ref[:, lanes]
        ends, carry_out = _scan_slab(decay[sl], h[sl], carry, reverse)
        if reverse:
            starts = jnp.where(sub == SUBLANES - 1, carry, pltpu.roll(ends, SUBLANES - 1, 0))
        else:
            starts = jnp.where(sub == 0, carry, pltpu.roll(ends, 1, 0))
        carry_ref[:, lanes] = carry_out
        for s in range(SUBLANES):
            src = pl.ds(s * SCAN_PITCH, seg)
            h_ref[s * seg:(s + 1) * seg, lanes] = u_s[sl, src, :] + a_s[sl, src, :] * starts[s:s + 1, :]


def _odd_scan(xr, wg, bg, lam, state, j, geom, direction):
    n, d = xr.shape
    ts = SCAN_TILE
    n_tiles = n // ts
    ctx_tiles = geom.n_ctx // ts
    tpc = geom.ctx_len // ts
    tpl = geom.lat_len // ts
    reverse = direction == 1

    def tile(nn):
        return n_tiles - 1 - nn if reverse else nn

    def seq_of(g):
        return jnp.where(g < ctx_tiles, g // tpc, geom.n_ctx_seq + (g - ctx_tiles) // tpl)

    def first_of(nn):
        g = tile(nn)
        pos = jnp.where(g < ctx_tiles, g % tpc, (g - ctx_tiles) % tpl)
        per = jnp.where(g < ctx_tiles, tpc, tpl)
        return (pos == per - 1 if reverse else pos == 0), g < ctx_tiles

    def lat_seq(g):
        return jnp.clip(seq_of(g) - geom.n_ctx_seq, 0, geom.n_lat_seq - 1)

    blk = pl.BlockSpec((ts, d), lambda nn: (tile(nn), 0))
    return pl.pallas_call(
        functools.partial(_odd_scan_kernel, first_of=first_of, reverse=reverse),
        grid=(n_tiles,),
        in_specs=[blk, _resident(wg.shape), _resident(bg.shape), _resident(lam.shape),
                  pl.BlockSpec((None, None, None, 1, d), lambda nn: (lat_seq(tile(nn)), j, direction, 0, 0))],
        out_specs=blk,
        out_shape=jax.ShapeDtypeStruct((n, d), F32),
        scratch_shapes=[pltpu.VMEM((1, d), F32)] + [pltpu.VMEM((d // LANES, SUBLANES * SCAN_PITCH, LANES), F32)] * 2,
        compiler_params=_cparams(1),
        name="odd_scan",
    )(xr, wg, bg, lam, state)


def _modulated_ext(xp_ref, x_ref, xn_ref, g_ref, shift, scale, first, last):
    g = g_ref[...]
    h_p = _rms(xp_ref[...], g) * (1.0 + scale) + shift
    h_c = _rms(x_ref[...], g) * (1.0 + scale) + shift
    h_n = _rms(xn_ref[...], g) * (1.0 + scale) + shift
    h_p = jnp.where(first, 0.0, h_p)
    h_n = jnp.where(last, 0.0, h_n)
    return jnp.concatenate([h_p, h_c, h_n], axis=0).astype(BF16)


def _ext(refs3):
    return jnp.concatenate([r[...] for r in refs3], axis=0)


def _mix_ffn_kernel(*refs, geom, final, kind):
    n_mix = 3 if kind == "odd" else 4
    x3 = refs[0:3]
    mix3 = [refs[3 + 3 * j:6 + 3 * j] for j in range(n_mix)]
    pos = 3 + 3 * n_mix
    mod_ref, g_ref = refs[pos:pos + 2]
    pos += 2
    if kind == "even":
        gn_ref = refs[pos]
        pos += 1
    wo_ref, wu_ref, cw_ref, cb_ref, wd_ref, fn_ref = refs[pos:pos + 6]
    out_refs = refs[pos + 6:-1]
    s_ref = refs[-1]

    i = pl.program_id(0)
    t = geom.tile
    first, last = geom.seq_edges(i)
    if kind == "odd":
        z = _ext(mix3[0]) * (_ext(mix3[1]) + _ext(mix3[2]))
    else:
        ob = _ext(mix3[1]) + _ext(mix3[2])
        gb = _ext(mix3[3])
        gn = gn_ref[...]
        z = jnp.concatenate(
            [_ext(mix3[0])] + [_rms(ob[:, hh * B_DV:(hh + 1) * B_DV], gn) * _silu(gb[:, hh * B_DV:(hh + 1) * B_DV])
                               for hh in range(B_HEADS)], axis=1)
    x_new = _ext(x3) + mod_ref[2:3, :] * _dot(z.astype(BF16), wo_ref[...])
    h = _rms(x_new, g_ref[...]) * (1.0 + mod_ref[4:5, :]) + mod_ref[3:4, :]
    row = lax.broadcasted_iota(jnp.int32, (t + 2 * HALO, 1), 0)
    beyond = (first & (row < HALO)) | (last & (row >= HALO + t))
    h_ext = jnp.where(beyond, 0.0, h).astype(BF16)
    n_chunks, fc = wd_ref.shape[0], wd_ref.shape[1]
    d_ff = n_chunks * fc
    n_slab = fc // LANES

    def project(c):
        for part in range(2):
            u_ext = _dot(h_ext, wu_ref[:, part * d_ff + c * fc:part * d_ff + (c + 1) * fc])
            for sl in range(n_slab):
                s_ref[c % 2, part * n_slab + sl] = u_ext[:, sl * LANES:(sl + 1) * LANES]

    acc = None
    project(0)
    for c in range(n_chunks):
        if c + 1 < n_chunks:
            project(c + 1)
        acts = []
        for sl in range(n_slab):
            cols = slice(sl * LANES, (sl + 1) * LANES)
            halves = []
            for part in range(2):
                slab = s_ref.at[c % 2, part * n_slab + sl]
                cw = cw_ref[part, c]
                halves.append(slab[pl.ds(HALO - 1, t), :] * cw[0:1, cols]
                              + slab[pl.ds(HALO, t), :] * cw[1:2, cols]
                              + slab[pl.ds(HALO + 1, t), :] * cw[2:3, cols]
                              + cb_ref[part, c][:, cols])
            acts.append((_silu(halves[0]) * halves[1]).astype(BF16))
        part_out = _dot(jnp.concatenate(acts, axis=1), wd_ref[c])
        acc = part_out if acc is None else acc + part_out
    y = x_new[HALO:HALO + t] + mod_ref[5:6, :] * acc
    if not final:
        out_refs[0][...] = y
    else:
        y = _rms(y, fn_ref[...])

        @pl.when(i < geom.ctx_tiles)
        def _():
            out_refs[0][...] = y

        @pl.when(i >= geom.ctx_tiles)
        def _():
            out_refs[1][...] = y


def _mix_ffn(x, mod, layer, g, mix_inputs, mix_params, wu, cw, cb, wd, fn, geom, final, kind):
    n, d = x.shape
    t = geom.tile
    n_slab = wd.shape[1] // LANES
    tiled = [x] + list(mix_inputs)
    in_specs = []
    operands = []
    for a in tiled:
        in_specs += _halo_specs(geom, a.shape[1])
        operands += [a, a, a]
    residents = [g] + list(mix_params) + [wu, cw, cb, wd, fn]
    in_specs += [_mod_spec(geom, layer, d)] + [_resident(a.shape) for a in residents]
    operands += [mod] + residents
    if final:
        ct = geom.ctx_tiles
        out_specs = [pl.BlockSpec((t, d), lambda i: (jnp.minimum(i, ct - 1), 0)),
                     pl.BlockSpec((t, d), lambda i: (jnp.maximum(i - ct, 0), 0))]
        out_shape = [jax.ShapeDtypeStruct((geom.n_ctx, d), F32), jax.ShapeDtypeStruct((geom.n_lat, d), F32)]
    else:
        out_specs = [_row_spec(geom, d)]
        out_shape = [jax.ShapeDtypeStruct((n, d), F32)]
    return pl.pallas_call(
        functools.partial(_mix_ffn_kernel, geom=geom, final=final, kind=kind),
        grid=(geom.tiles,),
        in_specs=in_specs,
        out_specs=out_specs,
        out_shape=out_shape,
        scratch_shapes=[pltpu.VMEM((2, 2 * n_slab, t + 2 * HALO, LANES), F32)],
        compiler_params=_cparams(1),
        name="mix_ffn_" + kind,
    )(*operands)


def _rope_tables(t_len):
    rows = t_len // GRID_W
    row = jnp.repeat(jnp.arange(rows, dtype=F32), GRID_W)
    col = jnp.tile(jnp.arange(GRID_W, dtype=F32), rows)
    n_freq = HEAD_DIM // 4
    inv_freq = jnp.power(ROPE_BASE, -jnp.arange(n_freq, dtype=F32) / n_freq)
    ang_r = row[:, None] * inv_freq
    ang_c = col[:, None] * inv_freq
    cos_h = jnp.concatenate([jnp.cos(ang_r)] * 2 + [jnp.cos(ang_c)] * 2, axis=1)
    sin_h = jnp.concatenate([-jnp.sin(ang_r), jnp.sin(ang_r), -jnp.sin(ang_c), jnp.sin(ang_c)], axis=1)
    reps = LANES // HEAD_DIM
    return jnp.tile(cos_h, (1, reps)), jnp.tile(sin_h, (1, reps))


def _block_diag_groups(w, grp):
    nb, bw, _ = w.shape
    per = grp // bw
    w = w.reshape(nb // per, per, bw, bw)
    eye = jnp.eye(per, dtype=w.dtype)
    return jnp.einsum('gpij,pq->gpiqj', w, eye).reshape(nb // per, grp, grp)


def kernel(x_prompt, x_sample, cache_attn_k, cache_attn_v, state_gla, state_rglru, c, c_ctx,
           norm_mix, norm_ffn, w_ada, b_ada,
           ev_w_in, ev_sink, ev_w_gate_f, ev_b_gate_f, ev_w_gate_b, ev_b_gate_b, ev_gla_norm, ev_w_out,
           od_w_in, od_conv_w, od_conv_b, od_w_a, od_b_a, od_w_i, od_b_i, od_lambda, od_w_out,
           ffn_w_up, ffn_conv_w, ffn_conv_b, ffn_w_down, final_norm):
    batch, seq, d = x_prompt.shape
    dec_batch, dec_seq, _ = x_sample.shape
    depth = w_ada.shape[0]
    geom = _Geom(batch, seq, dec_batch, dec_seq, min(ROW_TILE, dec_seq))
    geom_c = _Geom(batch, seq, dec_batch, dec_seq, min(CONV_TILE, seq))

    cond = jnp.concatenate([c_ctx[None, :], c], axis=0)
    cond = jnp.pad(cond, ((0, SUBLANES - cond.shape[0] % SUBLANES), (0, 0)))
    mod = _ada(cond, w_ada, b_ada)

    x = None
    cos_t, sin_t = _rope_tables(dec_seq)
    gla_consts = _gla_constants(GLA_CHUNK)
    akv = A_KV_HEADS * HEAD_DIM
    bqk = B_HEADS * B_DK
    d_ff = ffn_w_down.shape[1]
    n_ff = d_ff // FF_CHUNK

    new_k, new_v, new_gla, new_rg = [], [], [], []
    for layer in range(depth):
        j = layer // 2
        g_mix = norm_mix[layer][None, :]
        if layer % 2 == 0:
            w_in = ev_w_in[j]
            main = w_in.shape[1] - 2 * GATE_RANK
            w_in_p = jnp.pad(w_in, ((0, 0), (0, LANES - 2 * GATE_RANK))).astype(BF16)
            wg = jnp.zeros((LANES, 2 * bqk), F32)
            wg = wg.at[0:GATE_RANK, 0:bqk].set(ev_w_gate_f[j])
            wg = wg.at[GATE_RANK:2 * GATE_RANK, bqk:2 * bqk].set(ev_w_gate_b[j]).astype(BF16)
            bg = jnp.concatenate([ev_b_gate_f[j], ev_b_gate_b[j]])[None, :]
            assert main % LANES == 0
            xs = [x] if layer else [x_prompt.reshape(batch * seq, d), x_sample.reshape(dec_batch * dec_seq, d)]
            outs = _even_in(xs, mod, layer, g_mix, w_in_p, wg, bg, cos_t, sin_t, geom)
            qa, ka, va, gb, gin = outs[:5]
            if not layer:
                x = outs[5]
            ck = cache_attn_k[:, j].reshape(dec_batch, -1, akv)
            cv = cache_attn_v[:, j].reshape(dec_batch, -1, akv)
            oa = _attention(qa, ka, va, ev_sink[j], ck, cv, geom)
            o_f, o_b, s_f, s_b = _gla(gin, state_gla, j, gla_consts, geom)
            kind, mix_inputs = "even", [oa, o_f, o_b, gb]
            mix_params = [ev_gla_norm[j][None, :], ev_w_out[j].astype(BF16)]
            new_k.append(ka[:geom.n_ctx].reshape(batch, seq, A_KV_HEADS, HEAD_DIM))
            new_v.append(va[:geom.n_ctx].reshape(batch, seq, A_KV_HEADS, HEAD_DIM))
            new_gla.append(jnp.stack([s_f, s_b], axis=1))
        else:
            w_in = od_w_in[j].astype(BF16)
            d_rnn = w_in.shape[1] // 2
            grp = 2 * LANES
            wa = jnp.stack([_block_diag_groups(od_w_a[j, dr], grp) for dr in range(2)])
            wi = jnp.stack([_block_diag_groups(od_w_i[j, dr], grp) for dr in range(2)])
            wgate = (0.5 * jnp.concatenate([wa, wi], axis=-1)).astype(BF16)
            ba = od_b_a[j].reshape(2, d_rnn // grp, 1, grp)
            bi = od_b_i[j].reshape(2, d_rnn // grp, 1, grp)
            bgate = 0.5 * jnp.concatenate([ba, bi], axis=-1)
            gy, xr = _odd_in(x, mod, layer, g_mix, w_in, od_conv_w[j], od_conv_b[j][None, :], geom_c)
            h0 = state_rglru.reshape(dec_batch, -1, 2, 1, d_rnn)
            h_f, h_b = [_odd_scan(xr, wgate[dr], bgate[dr], od_lambda[j, dr][None, :], h0, j, geom, dr)
                        for dr in range(2)]
            kind, mix_inputs, mix_params = "odd", [gy, h_f, h_b], [od_w_out[j].astype(BF16)]
            new_rg.append(jnp.stack([h_f[seq - 1:geom.n_ctx:seq], h_b[0:geom.n_ctx:seq]], axis=1))
        wu = ffn_w_up[layer].astype(BF16)
        cw = ffn_conv_w[layer].reshape(FFN_CONV_W, 2, n_ff, FF_CHUNK).transpose(1, 2, 0, 3)
        cb = ffn_conv_b[layer].reshape(2, n_ff, 1, FF_CHUNK)
        wd = ffn_w_down[layer].astype(BF16).reshape(n_ff, FF_CHUNK, d)
        outs = _mix_ffn(x, mod, layer, norm_ffn[layer][None, :], mix_inputs, mix_params, wu, cw, cb, wd,
                        final_norm[None, :], geom_c, final=(layer == depth - 1), kind=kind)
        x = outs[0]

    y_prompt = outs[0].reshape(batch, seq, d)
    y_sample = outs[1].reshape(dec_batch, dec_seq, d)
    return (y_prompt, y_sample, jnp.stack(new_k, axis=1), jnp.stack(new_v, axis=1),
            jnp.stack(new_gla, axis=1), jnp.stack(new_rg, axis=1))
```

```python
import functools

import jax
import jax.numpy as jnp
import numpy as np
from jax import lax
from jax.experimental import pallas as pl
from jax.experimental.pallas import tpu as pltpu

F32 = jnp.float32
BF16 = jnp.bfloat16

EPS = 1e-6
NEG_INF = -1e30
GRID_W = 64
A_HEADS = 8
A_KV_HEADS = 2
HEAD_DIM = 64
WINDOW = 128
ATT_BLOCK = 128
ROPE_BASE = 10000.0
B_HEADS = 4
B_DK = 64
B_DV = 128
GATE_RANK = 16
GATE_NORMALIZER = 16.0
RG_BLOCKS = 16
RG_C = 8.0
CONV_W = 4
FFN_CONV_W = 3

LANES = 128
SUBLANES = 8
HALO = SUBLANES
ROW_TILE = 1024
CONV_TILE = 256
GLA_CHUNK = 128
GLA_STEP = 256
SCAN_TILE = 256
SCAN_PITCH = SCAN_TILE // SUBLANES + 4
FF_CHUNK = 256
VMEM_LIMIT = 56 * 1024 * 1024


def _cparams(n_axes):
    return pltpu.CompilerParams(dimension_semantics=("arbitrary",) * n_axes,
                                vmem_limit_bytes=VMEM_LIMIT)


def _resident(shape):
    nd = len(shape)
    return pl.BlockSpec(shape, lambda *_: (0,) * nd, pipeline_mode=pl.Buffered(1))


def _silu(x):
    return x * jax.nn.sigmoid(x)


def _rms(x, g):
    return x * lax.rsqrt(jnp.mean(x * x, axis=-1, keepdims=True) + EPS) * g


def _dot(a, b):
    return jnp.dot(a, b, preferred_element_type=F32)


def _dot_nt(a, b):
    return lax.dot_general(a, b, (((1,), (1,)), ((), ())), preferred_element_type=F32)


def _dot_tn(a, b):
    return lax.dot_general(a, b, (((0,), (0,)), ((), ())), preferred_element_type=F32)


def _ada_kernel(c_ref, w_ref, b_ref, o_ref):
    s = _silu(c_ref[...])
    o_ref[...] = _dot(s.astype(BF16), w_ref[...].astype(BF16)) + b_ref[...]


def _ada(cond, w_ada, b_ada):
    depth, d, d6 = w_ada.shape
    nb = 4
    cb = d6 // nb
    rows = cond.shape[0]
    out = pl.pallas_call(
        _ada_kernel,
        grid=(depth, nb),
        in_specs=[pl.BlockSpec((rows, d), lambda l, j: (0, 0)),
                  pl.BlockSpec((None, d, cb), lambda l, j: (l, 0, j)),
                  pl.BlockSpec((None, 1, cb), lambda l, j: (l, 0, j))],
        out_specs=pl.BlockSpec((None, rows, cb), lambda l, j: (l, 0, j)),
        out_shape=jax.ShapeDtypeStruct((depth, rows, d6), F32),
        compiler_params=_cparams(2),
        name="ada",
    )(cond, w_ada, b_ada.reshape(depth, 1, d6))
    return out.reshape(depth, rows, 6, d)


class _Geom:
    def __init__(self, n_ctx_seq, ctx_len, n_lat_seq, lat_len, tile):
        assert ctx_len % tile == 0 or tile % ctx_len == 0
        assert lat_len % tile == 0
        self.n_ctx_seq, self.ctx_len, self.n_lat_seq, self.lat_len = n_ctx_seq, ctx_len, n_lat_seq, lat_len
        self.tile = tile
        self.n_ctx = n_ctx_seq * ctx_len
        self.n_lat = n_lat_seq * lat_len
        self.n = self.n_ctx + self.n_lat
        assert self.n_ctx % tile == 0
        self.ctx_tiles = self.n_ctx // tile
        self.lat_tiles_per_seq = lat_len // tile
        self.tiles = self.n // tile

    def mod_row(self, i):
        return jnp.where(i < self.ctx_tiles, 0, 1 + (i - self.ctx_tiles) // self.lat_tiles_per_seq)

    def lat_tile(self, i):
        return jnp.where(i < self.ctx_tiles, 0, (i - self.ctx_tiles) % self.lat_tiles_per_seq)

    def seq_edges(self, i):
        assert self.ctx_len % self.tile == 0
        is_ctx = i < self.ctx_tiles
        seq_len = jnp.where(is_ctx, self.ctx_len, self.lat_len)
        start = (i * self.tile - jnp.where(is_ctx, 0, self.n_ctx)) % seq_len
        return start == 0, start + self.tile == seq_len


def _halo_specs(geom, d):
    t = geom.tile
    per = t // HALO
    last = geom.n // HALO - 1
    return [pl.BlockSpec((HALO, d), lambda i: (jnp.maximum(i * per - 1, 0), 0)),
            pl.BlockSpec((t, d), lambda i: (i, 0)),
            pl.BlockSpec((HALO, d), lambda i: (jnp.minimum((i + 1) * per, last), 0))]


def _mod_spec(geom, layer, d):
    return pl.BlockSpec((None, None, 6, d), lambda i: (layer, geom.mod_row(i), 0, 0))


def _row_spec(geom, width, col=0):
    return pl.BlockSpec((geom.tile, width), lambda i: (i, col))


def _rope_apply(t, cos, sin):
    lane = lax.broadcasted_iota(jnp.int32, t.shape, 1)
    swapped = jnp.where((lane & 16) == 0, pltpu.roll(t, LANES - 16, 1), pltpu.roll(t, 16, 1))
    return t * cos + swapped * sin


def _even_in_kernel(*refs, geom, split_x):
    i = pl.program_id(0)
    is_lat = i >= geom.ctx_tiles
    if split_x:
        xc_ref, xl_ref = refs[0:2]
        xcat_ref = refs[-1]
        refs = refs[2:-1]
        x = jnp.where(is_lat, xl_ref[...], xc_ref[...])
        xcat_ref[...] = x
    else:
        x = refs[0][...]
        refs = refs[1:]
    mod_ref, g_ref, w_ref, wg_ref, bg_ref, cos_ref, sin_ref, qa_ref, ka_ref, va_ref, gb_ref, gin_ref = refs
    h = _rms(x, g_ref[...]) * (1.0 + mod_ref[1:2, :]) + mod_ref[0:1, :]
    p = _dot(h.astype(BF16), w_ref[...])
    aq = A_HEADS * HEAD_DIM
    akv = A_KV_HEADS * HEAD_DIM
    bqk = B_HEADS * B_DK
    bv = B_HEADS * B_DV
    o = 0
    q_a = p[:, o:o + aq] * (HEAD_DIM ** -0.5); o += aq
    k_a = p[:, o:o + akv]; o += akv
    va_ref[...] = p[:, o:o + akv]; o += akv
    gin_ref[:, 0:bqk] = p[:, o:o + bqk] * (B_DK ** -0.5); o += bqk
    gin_ref[:, bqk:2 * bqk] = p[:, o:o + bqk]; o += bqk
    gin_ref[:, 2 * bqk:2 * bqk + bv] = p[:, o:o + bv]; o += bv
    gb_ref[...] = p[:, o:o + bv]; o += bv
    r = p[:, o:o + LANES]
    z = _dot(r.astype(BF16), wg_ref[...]) + bg_ref[...]
    gin_ref[:, 2 * bqk + bv:4 * bqk + bv] = jax.nn.log_sigmoid(z) * (1.0 / GATE_NORMALIZER)

    cos, sin = cos_ref[...], sin_ref[...]
    for j in range(aq // LANES):
        blk = q_a[:, j * LANES:(j + 1) * LANES]
        qa_ref[:, j * LANES:(j + 1) * LANES] = jnp.where(is_lat, _rope_apply(blk, cos, sin), blk).astype(BF16)
    for j in range(akv // LANES):
        blk = k_a[:, j * LANES:(j + 1) * LANES]
        ka_ref[:, j * LANES:(j + 1) * LANES] = jnp.where(is_lat, _rope_apply(blk, cos, sin), blk)


def _even_in(xs, mod, layer, g, w_in_p, wg, bg, cos_t, sin_t, geom):
    d = xs[0].shape[1]
    n = geom.n
    t = geom.tile
    aq = A_HEADS * HEAD_DIM
    akv = A_KV_HEADS * HEAD_DIM
    bqk = B_HEADS * B_DK
    bv = B_HEADS * B_DV
    split_x = len(xs) == 2
    ct = geom.ctx_tiles
    if split_x:
        x_specs = [pl.BlockSpec((t, d), lambda i: (jnp.minimum(i, ct - 1), 0)),
                   pl.BlockSpec((t, d), lambda i: (jnp.maximum(i - ct, 0), 0))]
    else:
        x_specs = [_row_spec(geom, d)]
    outs = [(aq, BF16), (akv, F32), (akv, F32), (bv, F32), (4 * bqk + bv, F32)] + ([(d, F32)] if split_x else [])
    return pl.pallas_call(
        functools.partial(_even_in_kernel, geom=geom, split_x=split_x),
        grid=(geom.tiles,),
        in_specs=x_specs + [_mod_spec(geom, layer, d), _resident(g.shape),
                            _resident(w_in_p.shape), _resident(wg.shape), _resident(bg.shape),
                            pl.BlockSpec((t, LANES), lambda i: (geom.lat_tile(i), 0)),
                            pl.BlockSpec((t, LANES), lambda i: (geom.lat_tile(i), 0))],
        out_specs=[_row_spec(geom, w) for w, _ in outs],
        out_shape=[jax.ShapeDtypeStruct((n, w), dt) for w, dt in outs],
        compiler_params=_cparams(1),
        name="even_in",
    )(*xs, mod, g, w_in_p, wg, bg, cos_t, sin_t)


def _attend(q, k, v, mask_fn, sink_ref, o_ref):
    nq, nk = q.shape[0], k.shape[0]
    lane = lax.broadcasted_iota(jnp.int32, k.shape, 1)
    lo = lane < HEAD_DIM
    k_sw = pltpu.roll(k, HEAD_DIM, 1)
    v_sw = pltpu.roll(v, HEAD_DIM, 1)
    kk = [[jnp.where(lo, k, 0.0).astype(BF16), jnp.where(lo, 0.0, k_sw).astype(BF16)],
          [jnp.where(lo, k_sw, 0.0).astype(BF16), jnp.where(lo, 0.0, k).astype(BF16)]]
    vv = [[jnp.where(lo, v, 0.0).astype(BF16), jnp.where(lo, 0.0, v_sw).astype(BF16)],
          [jnp.where(lo, v_sw, 0.0).astype(BF16), jnp.where(lo, 0.0, v).astype(BF16)]]
    group = A_HEADS // A_KV_HEADS
    mask = None
    if mask_fn is not None:
        mask = mask_fn(lax.broadcasted_iota(jnp.int32, (nq, nk), 0), lax.broadcasted_iota(jnp.int32, (nq, nk), 1))
    heads = range(A_HEADS)
    qb = [q[:, pair * LANES:(pair + 1) * LANES].astype(BF16) for pair in range(A_HEADS // 2)]
    s = [_dot_nt(qb[h // 2], kk[h // group][h % 2]) for h in heads]
    if mask is not None:
        s = [jnp.where(mask, sh, NEG_INF) for sh in s]
    m = [jnp.maximum(jnp.max(s[h], axis=1, keepdims=True), sink_ref[h]) for h in heads]
    e = [jnp.exp(s[h] - m[h]) for h in heads]
    inv = [1.0 / (jnp.sum(e[h], axis=1, keepdims=True) + jnp.exp(sink_ref[h] - m[h])) for h in heads]
    o = [_dot((e[h] * inv[h]).astype(BF16), vv[h // group][h % 2]) for h in heads]
    for pair in range(A_HEADS // 2):
        o_ref[:, pair * LANES:(pair + 1) * LANES] = o[2 * pair] + o[2 * pair + 1]


def _attn_kernel(sink_ref, q_ref, kp_ref, kc_ref, kn_ref, vp_ref, vc_ref, vn_ref, kx_ref, vx_ref, o_ref,
                 *, ctx_blocks, lat_blocks):
    g = pl.program_id(0)
    blk = ATT_BLOCK

    @pl.when(g < ctx_blocks)
    def _():
        even = g % 2 == 0
        k = jnp.concatenate([kc_ref[...], jnp.where(even, kn_ref[...], kp_ref[...])], axis=0)
        v = jnp.concatenate([vc_ref[...], jnp.where(even, vn_ref[...], vp_ref[...])], axis=0)
        _attend(q_ref[...], k, v, None, sink_ref, o_ref)

    @pl.when(g >= ctx_blocks)
    def _():
        i = (g - ctx_blocks) % lat_blocks
        k = jnp.concatenate([kp_ref[...], kc_ref[...], kn_ref[...], kx_ref[...]], axis=0)
        v = jnp.concatenate([vp_ref[...], vc_ref[...], vn_ref[...], vx_ref[...]], axis=0)

        def mask_fn(qi, kj):
            in_win = (jnp.abs(kj - blk - qi) <= WINDOW) & (kj < 3 * blk)
            in_win = in_win & ((kj >= blk) | (i > 0)) & ((kj < 2 * blk) | (i < lat_blocks - 1))
            return in_win | (kj >= 3 * blk)

        _attend(q_ref[...], k, v, mask_fn, sink_ref, o_ref)


def _attention(qa, ka, va, sink, cache_k, cache_v, geom):
    aq = qa.shape[1]
    akv = ka.shape[1]
    blk = ATT_BLOCK
    assert geom.ctx_len == 2 * blk and WINDOW <= blk
    n_blocks = geom.n // blk
    ctx_blocks = geom.n_ctx // blk
    lat_blocks = geom.lat_len // blk
    past = cache_k.shape[1]

    def lat_seq(g):
        return jnp.clip((g - ctx_blocks) // lat_blocks, 0, geom.n_lat_seq - 1)

    kv_spec = lambda f: pl.BlockSpec((blk, akv), f)
    prv = lambda g: (jnp.maximum(g - 1, 0), 0)
    cur = lambda g: (g, 0)
    nxt = lambda g: (jnp.minimum(g + 1, n_blocks - 1), 0)
    cache_spec = pl.BlockSpec((None, past, akv), lambda g: (lat_seq(g), 0, 0))
    return pl.pallas_call(
        functools.partial(_attn_kernel, ctx_blocks=ctx_blocks, lat_blocks=lat_blocks),
        grid=(n_blocks,),
        in_specs=[pl.BlockSpec(memory_space=pltpu.SMEM), pl.BlockSpec((blk, aq), cur),
                  kv_spec(prv), kv_spec(cur), kv_spec(nxt),
                  kv_spec(prv), kv_spec(cur), kv_spec(nxt),
                  cache_spec, cache_spec],
        out_specs=pl.BlockSpec((blk, aq), cur),
        out_shape=jax.ShapeDtypeStruct((geom.n, aq), F32),
        compiler_params=_cparams(1),
        name="attn",
    )(sink, qa, ka, ka, ka, va, va, va, cache_k, cache_v)


def _gla_constants(c):
    levels = int(np.log2(c))
    i_idx = np.arange(c)[:, None]
    j_idx = np.arange(c)[None, :]
    lv = np.full((c, c), -1, np.int32)
    lv[np.arange(c), np.arange(c)] = levels
    for lvl in range(levels):
        s = c >> (lvl + 1)
        same_pair = (i_idx // (2 * s)) == (j_idx // (2 * s))
        lv[same_pair & ((i_idx % (2 * s)) >= s) & ((j_idx % (2 * s)) < s)] = lvl
    lv2 = np.stack([np.tile(lv, (1, 2)), np.tile(lv[::-1, ::-1], (1, 2))])
    tri = np.stack([j_idx <= i_idx, j_idx >= i_idx]).astype(np.float32)
    return jnp.asarray(tri, BF16), jnp.asarray(lv2)


def _gla_pair(q, k, v, la, tri, lv2, state, out, reverse):
    c = q.shape[0]
    levels = int(np.log2(c))
    groups = c // SUBLANES
    la_hi = la.astype(BF16)
    rem = la - la_hi.astype(F32)
    la_mid = rem.astype(BF16)
    la_lo = (rem - la_mid.astype(F32)).astype(BF16)
    cum = (_dot(tri, la_hi) + _dot(tri, la_mid)) + _dot(tri, la_lo)
    edge = 0 if reverse else c - 1
    total = cum[edge:edge + 1, :]
    lane = lax.broadcasted_iota(jnp.int32, (c, LANES), 1)
    row = lax.broadcasted_iota(jnp.int32, (c, LANES), 0)
    lo = lane < B_DK

    def heads_on_rows(kt):
        return jnp.concatenate([jnp.where(lo, kt, 0.0), jnp.where(lo, 0.0, kt)], axis=0).astype(BF16)

    def rows_of(x, g):
        return x[g * SUBLANES:(g + 1) * SUBLANES]

    yield
    p = _dot_nt(q.astype(BF16), heads_on_rows(k))
    a = [jnp.where(rows_of(lv2, g) == levels, rows_of(p, g), 0.0) for g in range(groups)]
    for lvl in range(levels):
        yield
        s = c >> (lvl + 1)
        if s >= SUBLANES:
            q_parts, k_parts, q_groups = [], [], []
            zeros = jnp.zeros((s, LANES), F32)
            for b0 in range(0, c, 2 * s):
                first, second = slice(b0, b0 + s), slice(b0 + s, b0 + 2 * s)
                qs, ks, m = (first, second, b0 + s) if reverse else (second, first, b0 + s - 1)
                ref = cum[m:m + 1, :]
                q_parts.append(q[qs] * jnp.exp(cum[qs] - ref))
                k_blk = k[ks] * jnp.exp(ref - cum[ks])
                k_parts += [zeros, k_blk] if reverse else [k_blk, zeros]
                q_groups += list(range(qs.start // SUBLANES, qs.stop // SUBLANES))
            qt = jnp.concatenate(q_parts, axis=0)
            kt = jnp.concatenate(k_parts, axis=0)
        else:
            q_groups = list(range(groups))
            pos = row & (2 * s - 1)
            q_side = (pos < s) if reverse else (pos >= s)
            if s == 1:
                qt = jnp.where(q_side, q * jnp.exp(la), 0.0)
                kt = jnp.where(q_side, 0.0, k)
            else:
                cum3 = cum.reshape(groups, SUBLANES, LANES)
                sub = lax.broadcasted_iota(jnp.int32, cum3.shape, 1)
                ref = None
                for b0 in range(0, SUBLANES, 2 * s):
                    m = b0 + s if reverse else b0 + s - 1
                    r = cum3[:, m:m + 1, :]
                    ref = r if ref is None else jnp.where(sub >= b0, r, ref)
                e = jnp.exp(-jnp.abs(cum3 - ref)).reshape(c, LANES)
                qt = jnp.where(q_side, q * e, 0.0)
                kt = jnp.where(q_side, 0.0, k * e)
        p = _dot_nt(qt.astype(BF16), heads_on_rows(kt))
        for idx, g in enumerate(q_groups):
            a[g] = jnp.where(rows_of(lv2, g) == lvl, rows_of(p, idx), a[g])

    yield
    a_full = jnp.concatenate(a, axis=0).astype(BF16)
    v_b = v.astype(BF16)
    intra = [_dot(a_full[:, hh * c:(hh + 1) * c], v_b[:, hh * B_DV:(hh + 1) * B_DV]) for hh in range(2)]
    upd = _dot_tn(v_b, (k * jnp.exp(total - cum)).astype(BF16))
    q_in = (q * jnp.exp(cum)).astype(BF16)
    yield
    st = state[0]
    o_inter = _dot_nt(q_in, st.astype(BF16))
    out.append(jnp.concatenate([intra[hh] + o_inter[:, hh * B_DV:(hh + 1) * B_DV] for hh in range(2)], axis=1))
    row2 = lax.broadcasted_iota(jnp.int32, st.shape, 0)
    lane2 = lax.broadcasted_iota(jnp.int32, st.shape, 1)
    state[0] = jnp.where((row2 < B_DV) == (lane2 < B_DK), jnp.exp(total) * st + upd, 0.0)


def _gla_kernel(gin_f_ref, s0_f_ref, gin_b_ref, s0_b_ref, tri_ref, lv_ref,
                o_f_ref, o_b_ref, sfin_f_ref, sfin_b_ref, st_ref, *, first_last):
    n = pl.program_id(0)
    c = GLA_CHUNK
    bqk = B_HEADS * B_DK
    bv = B_HEADS * B_DV
    pairs = B_HEADS // 2
    row2 = lax.broadcasted_iota(jnp.int32, (2 * B_DV, LANES), 0)
    lane2 = lax.broadcasted_iota(jnp.int32, (2 * B_DV, LANES), 1)
    diag_blocks = (row2 < B_DV) == (lane2 < B_DK)
    dirs = ((gin_f_ref, s0_f_ref, o_f_ref, sfin_f_ref), (gin_b_ref, s0_b_ref, o_b_ref, sfin_b_ref))
    for d, (_, s0_ref, _, _) in enumerate(dirs):
        is_first, _, is_ctx = first_last(d, n)

        @pl.when(is_first)
        def _():
            keep = diag_blocks & jnp.logical_not(is_ctx)
            for pr in range(pairs):
                s0t = jnp.concatenate([s0_ref[2 * pr], s0_ref[2 * pr + 1]], axis=0).T
                st_ref[d, pr] = jnp.where(keep, jnp.concatenate([s0t, s0t], axis=0), 0.0)

    n_sub = gin_f_ref.shape[0] // c
    chains, states = [], {}
    for d, (gin_ref, _, o_ref, _) in enumerate(dirs):
        la0 = 2 * bqk + bv + d * bqk
        for pr in range(pairs):
            states[d, pr] = [st_ref[d, pr]]
            for ci in (reversed(range(n_sub)) if d == 1 else range(n_sub)):
                rows = slice(ci * c, (ci + 1) * c)
                out = []
                gen = _gla_pair(gin_ref[rows, pr * LANES:(pr + 1) * LANES],
                                gin_ref[rows, bqk + pr * LANES:bqk + (pr + 1) * LANES],
                                gin_ref[rows, 2 * bqk + pr * 2 * B_DV:2 * bqk + (pr + 1) * 2 * B_DV],
                                gin_ref[rows, la0 + pr * LANES:la0 + (pr + 1) * LANES],
                                tri_ref[d], lv_ref[d], states[d, pr], out, reverse=(d == 1))
                chains.append((gen, out, o_ref, rows, pr))
    running = True
    while running:
        running = False
        for gen, _, _, _, _ in chains:
            for _ in gen:
                running = True
                break
    for _, out, o_ref, rows, pr in chains:
        o_ref[rows, pr * 2 * B_DV:(pr + 1) * 2 * B_DV] = out[0]
    for (d, pr), state in states.items():
        st_ref[d, pr] = state[0]

    for d, (_, _, _, sfin_ref) in enumerate(dirs):
        _, is_last, is_ctx = first_last(d, n)

        @pl.when(is_last & is_ctx)
        def _():
            for pr in range(pairs):
                st = st_ref[d, pr]
                for hh in range(2):
                    blk = st[hh * B_DV:(hh + 1) * B_DV, :].T
                    sfin_ref[2 * pr + hh] = blk[hh * B_DK:(hh + 1) * B_DK, :]


def _gla(gin, state, j, consts, geom):
    n = gin.shape[0]
    c = min(GLA_STEP, geom.ctx_len)
    assert c % GLA_CHUNK == 0
    tri, lv2 = consts
    n_chunks = n // c
    ctx_chunks = geom.n_ctx // c
    cpc = geom.ctx_len // c
    cpl = geom.lat_len // c
    bv = B_HEADS * B_DV

    def bwd(nn):
        return n_chunks - 1 - nn

    def seq_of(g):
        return jnp.where(g < ctx_chunks, g // cpc, geom.n_ctx_seq + (g - ctx_chunks) // cpl)

    def first_last(d, nn):
        g = bwd(nn) if d == 1 else nn
        pos = jnp.where(g < ctx_chunks, g % cpc, (g - ctx_chunks) % cpl)
        per = jnp.where(g < ctx_chunks, cpc, cpl)
        at_start, at_end = pos == 0, pos == per - 1
        return ((at_end, at_start) if d == 1 else (at_start, at_end)) + (g < ctx_chunks,)

    def lat_seq(g):
        return jnp.clip(seq_of(g) - geom.n_ctx_seq, 0, geom.n_lat_seq - 1)

    def ctx_seq(g):
        return jnp.minimum(seq_of(g), geom.n_ctx_seq - 1)

    def specs(idx, d):
        return [pl.BlockSpec((c, gin.shape[1]), lambda nn: (idx(nn), 0)),
                pl.BlockSpec((None, None, None, B_HEADS, B_DK, B_DV), lambda nn: (lat_seq(idx(nn)), j, d, 0, 0, 0))]

    fwd = lambda nn: nn
    sfin_shape = jax.ShapeDtypeStruct((geom.n_ctx_seq, B_HEADS, B_DK, B_DV), F32)
    return pl.pallas_call(
        functools.partial(_gla_kernel, first_last=first_last),
        grid=(n_chunks,),
        in_specs=specs(fwd, 0) + specs(bwd, 1) + [_resident(tri.shape), _resident(lv2.shape)],
        out_specs=[pl.BlockSpec((c, bv), lambda nn: (nn, 0)),
                   pl.BlockSpec((c, bv), lambda nn: (bwd(nn), 0)),
                   pl.BlockSpec((None, B_HEADS, B_DK, B_DV), lambda nn: (ctx_seq(nn), 0, 0, 0)),
                   pl.BlockSpec((None, B_HEADS, B_DK, B_DV), lambda nn: (ctx_seq(bwd(nn)), 0, 0, 0))],
        out_shape=[jax.ShapeDtypeStruct((n, bv), F32), jax.ShapeDtypeStruct((n, bv), F32), sfin_shape, sfin_shape],
        scratch_shapes=[pltpu.VMEM((2, B_HEADS // 2, 2 * B_DV, LANES), F32)],
        compiler_params=_cparams(1),
        name="gla",
    )(gin, state, gin, state, tri, lv2)


def _odd_in_kernel(xp_ref, x_ref, xn_ref, mod_ref, g_ref, w_ref, cw_ref, cb_ref, gy_ref, xr_ref, s_ref, *, geom):
    i = pl.program_id(0)
    t = geom.tile
    first, last = geom.seq_edges(i)
    h_ext = _modulated_ext(xp_ref, x_ref, xn_ref, g_ref, mod_ref[0:1, :], mod_ref[1:2, :], first, last)
    yx_ext = _dot(h_ext, w_ref[...])
    d_rnn = gy_ref.shape[1]
    gy_ref[...] = jax.nn.gelu(yx_ext[HALO:HALO + t, 0:d_rnn])
    left = CONV_W // 2
    for sl in range(d_rnn // LANES):
        cols = slice(sl * LANES, (sl + 1) * LANES)
        s_ref[sl] = yx_ext[:, d_rnn + sl * LANES:d_rnn + (sl + 1) * LANES]
        xr = cb_ref[:, cols]
        for tap in range(CONV_W):
            xr = xr + s_ref[sl, pl.ds(HALO - left + tap, t), :] * cw_ref[tap:tap + 1, cols]
        xr_ref[:, cols] = xr


def _odd_in(x, mod, layer, g, w_in, cw, cb, geom):
    n, d = x.shape
    t = geom.tile
    d_rnn = w_in.shape[1] // 2
    return pl.pallas_call(
        functools.partial(_odd_in_kernel, geom=geom),
        grid=(geom.tiles,),
        in_specs=_halo_specs(geom, d) + [_mod_spec(geom, layer, d)] +
                 [_resident(a.shape) for a in (g, w_in, cw, cb)],
        out_specs=[_row_spec(geom, d_rnn), _row_spec(geom, d_rnn)],
        out_shape=[jax.ShapeDtypeStruct((n, d_rnn), F32), jax.ShapeDtypeStruct((n, d_rnn), F32)],
        scratch_shapes=[pltpu.VMEM((d_rnn // LANES, t + 2 * HALO, LANES), F32)],
        compiler_params=_cparams(1),
        name="odd_in",
    )(x, x, x, mod, g, w_in, cw, cb)


def _scan_slab(a, u, carry, reverse):
    rows = a.shape[0]
    groups = rows // SUBLANES
    a3 = a.reshape(groups, SUBLANES, LANES)
    u3 = u.reshape(groups, SUBLANES, LANES)
    sub = lax.broadcasted_iota(jnp.int32, a3.shape, 1)
    k = 1
    while k < SUBLANES:
        shift = SUBLANES - k if reverse else k
        ok = sub < SUBLANES - k if reverse else sub >= k
        a_s = jnp.where(ok, pltpu.roll(a3, shift, 1), 1.0)
        u_s = jnp.where(ok, pltpu.roll(u3, shift, 1), 0.0)
        u3 = a3 * u_s + u3
        a3 = a3 * a_s
        k *= 2
    hs = [None] * groups
    edge = 0 if reverse else SUBLANES - 1
    for gi in (reversed(range(groups)) if reverse else range(groups)):
        h = a3[gi] * carry + u3[gi]
        hs[gi] = h
        carry = h[edge:edge + 1, :]
    return jnp.concatenate(hs, axis=0), carry


def _odd_scan_kernel(xr_ref, wg_ref, bg_ref, lam_ref, h0_ref, h_ref, carry_ref, a_s, u_s, *, first_of, reverse):
    n = pl.program_id(0)
    is_first, is_ctx = first_of(n)

    @pl.when(is_first)
    def _():
        carry_ref[...] = jnp.where(is_ctx, 0.0, h0_ref[...])

    rows = xr_ref.shape[0]
    seg = rows // SUBLANES
    xr = xr_ref[...]
    xr_b = xr.astype(BF16)
    half_scale = (0.5 * RG_C) * jax.nn.log_sigmoid(lam_ref[...])
    n_grp, grp = wg_ref.shape[0], wg_ref.shape[1]
    n_slab = xr_ref.shape[1] // LANES
    for gi in range(n_grp):
        cols = slice(gi * grp, (gi + 1) * grp)
        th = jnp.tanh(_dot(xr_b[:, cols], wg_ref[gi]) + bg_ref[gi])
        log_a = half_scale[:, cols] * th[:, :grp] + half_scale[:, cols]
        i_gate = 0.5 * th[:, grp:] + 0.5
        a = jnp.exp(log_a)
        u = jnp.sqrt(jnp.tanh(log_a) * (-1.0 - a * a)) * (i_gate * xr[:, cols])
        for j in range(grp // LANES):
            sl = gi * (grp // LANES) + j
            for s in range(SUBLANES):
                a_s[sl, pl.ds(s * SCAN_PITCH, seg), :] = a[s * seg:(s + 1) * seg, j * LANES:(j + 1) * LANES]
                u_s[sl, pl.ds(s * SCAN_PITCH, seg), :] = u[s * seg:(s + 1) * seg, j * LANES:(j + 1) * LANES]

    h = [jnp.zeros((SUBLANES, LANES), F32)] * n_slab
    decay = [jnp.ones((SUBLANES, LANES), F32)] * n_slab
    for t in (reversed(range(seg)) if reverse else range(seg)):
        for sl in range(n_slab):
            at = a_s[sl, pl.ds(t, SUBLANES, stride=SCAN_PITCH), :]
            ut = u_s[sl, pl.ds(t, SUBLANES, stride=SCAN_PITCH), :]
            h[sl] = at * h[sl] + ut
            decay[sl] = at * decay[sl]
            u_s[sl, pl.ds(t, SUBLANES, stride=SCAN_PITCH), :] = h[sl]
            a_s[sl, pl.ds(t, SUBLANES, stride=SCAN_PITCH), :] = decay[sl]

    sub = lax.broadcasted_iota(jnp.int32, (SUBLANES, LANES), 0)
    for sl in range(n_slab):
        lanes = slice(sl * LANES, (sl + 1) * LANES)
        carry = carry_ref[:, lanes]
        ends, carry_out = _scan_slab(decay[sl], h[sl], carry, reverse)
        if reverse:
            starts = jnp.where(sub == SUBLANES - 1, carry, pltpu.roll(ends, SUBLANES - 1, 0))
        else:
            starts = jnp.where(sub == 0, carry, pltpu.roll(ends, 1, 0))
        carry_ref[:, lanes] = carry_out
        for s in range(SUBLANES):
            src = pl.ds(s * SCAN_PITCH, seg)
            h_ref[s * seg:(s + 1) * seg, lanes] = u_s[sl, src, :] + a_s[sl, src, :] * starts[s:s + 1, :]


def _odd_scan(xr, wg, bg, lam, state, j, geom, direction):
    n, d = xr.shape
    ts = SCAN_TILE
    n_tiles = n // ts
    ctx_tiles = geom.n_ctx // ts
    tpc = geom.ctx_len // ts
    tpl = geom.lat_len // ts
    reverse = direction == 1

    def tile(nn):
        return n_tiles - 1 - nn if reverse else nn

    def seq_of(g):
        return jnp.where(g < ctx_tiles, g // tpc, geom.n_ctx_seq + (g - ctx_tiles) // tpl)

    def first_of(nn):
        g = tile(nn)
        pos = jnp.where(g < ctx_tiles, g % tpc, (g - ctx_tiles) % tpl)
        per = jnp.where(g < ctx_tiles, tpc, tpl)
        return (pos == per - 1 if reverse else pos == 0), g < ctx_tiles

    def lat_seq(g):
        return jnp.clip(seq_of(g) - geom.n_ctx_seq, 0, geom.n_lat_seq - 1)

    blk = pl.BlockSpec((ts, d), lambda nn: (tile(nn), 0))
    return pl.pallas_call(
        functools.partial(_odd_scan_kernel, first_of=first_of, reverse=reverse),
        grid=(n_tiles,),
        in_specs=[blk, _resident(wg.shape), _resident(bg.shape), _resident(lam.shape),
                  pl.BlockSpec((None, None, None, 1, d), lambda nn: (lat_seq(tile(nn)), j, direction, 0, 0))],
        out_specs=blk,
        out_shape=jax.ShapeDtypeStruct((n, d), F32),
        scratch_shapes=[pltpu.VMEM((1, d), F32)] + [pltpu.VMEM((d // LANES, SUBLANES * SCAN_PITCH, LANES), F32)] * 2,
        compiler_params=_cparams(1),
        name="odd_scan",
    )(xr, wg, bg, lam, state)


def _modulated_ext(xp_ref, x_ref, xn_ref, g_ref, shift, scale, first, last):
    g = g_ref[...]
    h_p = _rms(xp_ref[...], g) * (1.0 + scale) + shift
    h_c = _rms(x_ref[...], g) * (1.0 + scale) + shift
    h_n = _rms(xn_ref[...], g) * (1.0 + scale) + shift
    h_p = jnp.where(first, 0.0, h_p)
    h_n = jnp.where(last, 0.0, h_n)
    return jnp.concatenate([h_p, h_c, h_n], axis=0).astype(BF16)


def _ext(refs3):
    return jnp.concatenate([r[...] for r in refs3], axis=0)


def _mix_ffn_kernel(*refs, geom, final, kind):
    n_mix = 3 if kind == "odd" else 4
    x3 = refs[0:3]
    mix3 = [refs[3 + 3 * j:6 + 3 * j] for j in range(n_mix)]
    pos = 3 + 3 * n_mix
    mod_ref, g_ref = refs[pos:pos + 2]
    pos += 2
    if kind == "even":
        gn_ref = refs[pos]
        pos += 1
    wo_ref, wu_ref, cw_ref, cb_ref, wd_ref, fn_ref = refs[pos:pos + 6]
    out_refs = refs[pos + 6:-1]
    s_ref = refs[-1]

    i = pl.program_id(0)
    t = geom.tile
    first, last = geom.seq_edges(i)
    if kind == "odd":
        z = _ext(mix3[0]) * (_ext(mix3[1]) + _ext(mix3[2]))
    else:
        ob = _ext(mix3[1]) + _ext(mix3[2])
        gb = _ext(mix3[3])
        gn = gn_ref[...]
        z = jnp.concatenate(
            [_ext(mix3[0])] + [_rms(ob[:, hh * B_DV:(hh + 1) * B_DV], gn) * _silu(gb[:, hh * B_DV:(hh + 1) * B_DV])
                               for hh in range(B_HEADS)], axis=1)
    x_new = _ext(x3) + mod_ref[2:3, :] * _dot(z.astype(BF16), wo_ref[...])
    h = _rms(x_new, g_ref[...]) * (1.0 + mod_ref[4:5, :]) + mod_ref[3:4, :]
    row = lax.broadcasted_iota(jnp.int32, (t + 2 * HALO, 1), 0)
    beyond = (first & (row < HALO)) | (last & (row >= HALO + t))
    h_ext = jnp.where(beyond, 0.0, h).astype(BF16)
    n_chunks, fc = wd_ref.shape[0], wd_ref.shape[1]
    d_ff = n_chunks * fc
    n_slab = fc // LANES

    def project(c):
        for part in range(2):
            u_ext = _dot(h_ext, wu_ref[:, part * d_ff + c * fc:part * d_ff + (c + 1) * fc])
            for sl in range(n_slab):
                s_ref[c % 2, part * n_slab + sl] = u_ext[:, sl * LANES:(sl + 1) * LANES]

    acc = None
    project(0)
    for c in range(n_chunks):
        if c + 1 < n_chunks:
            project(c + 1)
        acts = []
        for sl in range(n_slab):
            cols = slice(sl * LANES, (sl + 1) * LANES)
            halves = []
            for part in range(2):
                slab = s_ref.at[c % 2, part * n_slab + sl]
                cw = cw_ref[part, c]
                halves.append(slab[pl.ds(HALO - 1, t), :] * cw[0:1, cols]
                              + slab[pl.ds(HALO, t), :] * cw[1:2, cols]
                              + slab[pl.ds(HALO + 1, t), :] * cw[2:3, cols]
                              + cb_ref[part, c][:, cols])
            acts.append((_silu(halves[0]) * halves[1]).astype(BF16))
        part_out = _dot(jnp.concatenate(acts, axis=1), wd_ref[c])
        acc = part_out if acc is None else acc + part_out
    y = x_new[HALO:HALO + t] + mod_ref[5:6, :] * acc
    if not final:
        out_refs[0][...] = y
    else:
        y = _rms(y, fn_ref[...])

        @pl.when(i < geom.ctx_tiles)
        def _():
            out_refs[0][...] = y

        @pl.when(i >= geom.ctx_tiles)
        def _():
            out_refs[1][...] = y


def _mix_ffn(x, mod, layer, g, mix_inputs, mix_params, wu, cw, cb, wd, fn, geom, final, kind):
    n, d = x.shape
    t = geom.tile
    n_slab = wd.shape[1] // LANES
    tiled = [x] + list(mix_inputs)
    in_specs = []
    operands = []
    for a in tiled:
        in_specs += _halo_specs(geom, a.shape[1])
        operands += [a, a, a]
    residents = [g] + list(mix_params) + [wu, cw, cb, wd, fn]
    in_specs += [_mod_spec(geom, layer, d)] + [_resident(a.shape) for a in residents]
    operands += [mod] + residents
    if final:
        ct = geom.ctx_tiles
        out_specs = [pl.BlockSpec((t, d), lambda i: (jnp.minimum(i, ct - 1), 0)),
                     pl.BlockSpec((t, d), lambda i: (jnp.maximum(i - ct, 0), 0))]
        out_shape = [jax.ShapeDtypeStruct((geom.n_ctx, d), F32), jax.ShapeDtypeStruct((geom.n_lat, d), F32)]
    else:
        out_specs = [_row_spec(geom, d)]
        out_shape = [jax.ShapeDtypeStruct((n, d), F32)]
    return pl.pallas_call(
        functools.partial(_mix_ffn_kernel, geom=geom, final=final, kind=kind),
        grid=(geom.tiles,),
        in_specs=in_specs,
        out_specs=out_specs,
        out_shape=out_shape,
        scratch_shapes=[pltpu.VMEM((2, 2 * n_slab, t + 2 * HALO, LANES), F32)],
        compiler_params=_cparams(1),
        name="mix_ffn_" + kind,
    )(*operands)


def _rope_tables(t_len):
    rows = t_len // GRID_W
    row = jnp.repeat(jnp.arange(rows, dtype=F32), GRID_W)
    col = jnp.tile(jnp.arange(GRID_W, dtype=F32), rows)
    n_freq = HEAD_DIM // 4
    inv_freq = jnp.power(ROPE_BASE, -jnp.arange(n_freq, dtype=F32) / n_freq)
    ang_r = row[:, None] * inv_freq
    ang_c = col[:, None] * inv_freq
    cos_h = jnp.concatenate([jnp.cos(ang_r)] * 2 + [jnp.cos(ang_c)] * 2, axis=1)
    sin_h = jnp.concatenate([-jnp.sin(ang_r), jnp.sin(ang_r), -jnp.sin(ang_c), jnp.sin(ang_c)], axis=1)
    reps = LANES // HEAD_DIM
    return jnp.tile(cos_h, (1, reps)), jnp.tile(sin_h, (1, reps))


def _block_diag_groups(w, grp):
    nb, bw, _ = w.shape
    per = grp // bw
    w = w.reshape(nb // per, per, bw, bw)
    eye = jnp.eye(per, dtype=w.dtype)
    return jnp.einsum('gpij,pq->gpiqj', w, eye).reshape(nb // per, grp, grp)


def kernel(x_prompt, x_sample, cache_attn_k, cache_attn_v, state_gla, state_rglru, c, c_ctx,
           norm_mix, norm_ffn, w_ada, b_ada,
           ev_w_in, ev_sink, ev_w_gate_f, ev_b_gate_f, ev_w_gate_b, ev_b_gate_b, ev_gla_norm, ev_w_out,
           od_w_in, od_conv_w, od_conv_b, od_w_a, od_b_a, od_w_i, od_b_i, od_lambda, od_w_out,
           ffn_w_up, ffn_conv_w, ffn_conv_b, ffn_w_down, final_norm):
    batch, seq, d = x_prompt.shape
    dec_batch, dec_seq, _ = x_sample.shape
    depth = w_ada.shape[0]
    geom = _Geom(batch, seq, dec_batch, dec_seq, min(ROW_TILE, dec_seq))
    geom_c = _Geom(batch, seq, dec_batch, dec_seq, min(CONV_TILE, seq))

    cond = jnp.concatenate([c_ctx[None, :], c], axis=0)
    cond = jnp.pad(cond, ((0, SUBLANES - cond.shape[0] % SUBLANES), (0, 0)))
    mod = _ada(cond, w_ada, b_ada)

    x = None
    cos_t, sin_t = _rope_tables(dec_seq)
    gla_consts = _gla_constants(GLA_CHUNK)
    akv = A_KV_HEADS * HEAD_DIM
    bqk = B_HEADS * B_DK
    d_ff = ffn_w_down.shape[1]
    n_ff = d_ff // FF_CHUNK

    new_k, new_v, new_gla, new_rg = [], [], [], []
    for layer in range(depth):
        j = layer // 2
        g_mix = norm_mix[layer][None, :]
        if layer % 2 == 0:
            w_in = ev_w_in[j]
            main = w_in.shape[1] - 2 * GATE_RANK
            w_in_p = jnp.pad(w_in, ((0, 0), (0, LANES - 2 * GATE_RANK))).astype(BF16)
            wg = jnp.zeros((LANES, 2 * bqk), F32)
            wg = wg.at[0:GATE_RANK, 0:bqk].set(ev_w_gate_f[j])
            wg = wg.at[GATE_RANK:2 * GATE_RANK, bqk:2 * bqk].set(ev_w_gate_b[j]).astype(BF16)
            bg = jnp.concatenate([ev_b_gate_f[j], ev_b_gate_b[j]])[None, :]
            assert main % LANES == 0
            xs = [x] if layer else [x_prompt.reshape(batch * seq, d), x_sample.reshape(dec_batch * dec_seq, d)]
            outs = _even_in(xs, mod, layer, g_mix, w_in_p, wg, bg, cos_t, sin_t, geom)
            qa, ka, va, gb, gin = outs[:5]
            if not layer:
                x = outs[5]
            ck = cache_attn_k[:, j].reshape(dec_batch, -1, akv)
            cv = cache_attn_v[:, j].reshape(dec_batch, -1, akv)
            oa = _attention(qa, ka, va, ev_sink[j], ck, cv, geom)
            o_f, o_b, s_f, s_b = _gla(gin, state_gla, j, gla_consts, geom)
            kind, mix_inputs = "even", [oa, o_f, o_b, gb]
            mix_params = [ev_gla_norm[j][None, :], ev_w_out[j].astype(BF16)]
            new_k.append(ka[:geom.n_ctx].reshape(batch, seq, A_KV_HEADS, HEAD_DIM))
            new_v.append(va[:geom.n_ctx].reshape(batch, seq, A_KV_HEADS, HEAD_DIM))
            new_gla.append(jnp.stack([s_f, s_b], axis=1))
        else:
            w_in = od_w_in[j].astype(BF16)
            d_rnn = w_in.shape[1] // 2
            grp = 2 * LANES
            wa = jnp.stack([_block_diag_groups(od_w_a[j, dr], grp) for dr in range(2)])
            wi = jnp.stack([_block_diag_groups(od_w_i[j, dr], grp) for dr in range(2)])
            wgate = (0.5 * jnp.concatenate([wa, wi], axis=-1)).astype(BF16)
            ba = od_b_a[j].reshape(2, d_rnn // grp, 1, grp)
            bi = od_b_i[j].reshape(2, d_rnn // grp, 1, grp)
            bgate = 0.5 * jnp.concatenate([ba, bi], axis=-1)
            gy, xr = _odd_in(x, mod, layer, g_mix, w_in, od_conv_w[j], od_conv_b[j][None, :], geom_c)
            h0 = state_rglru.reshape(dec_batch, -1, 2, 1, d_rnn)
            h_f, h_b = [_odd_scan(xr, wgate[dr], bgate[dr], od_lambda[j, dr][None, :], h0, j, geom, dr)
                        for dr in range(2)]
            kind, mix_inputs, mix_params = "odd", [gy, h_f, h_b], [od_w_out[j].astype(BF16)]
            new_rg.append(jnp.stack([h_f[seq - 1:geom.n_ctx:seq], h_b[0:geom.n_ctx:seq]], axis=1))
        wu = ffn_w_up[layer].astype(BF16)
        cw = ffn_conv_w[layer].reshape(FFN_CONV_W, 2, n_ff, FF_CHUNK).transpose(1, 2, 0, 3)
        cb = ffn_conv_b[layer].reshape(2, n_ff, 1, FF_CHUNK)
        wd = ffn_w_down[layer].astype(BF16).reshape(n_ff, FF_CHUNK, d)
        outs = _mix_ffn(x, mod, layer, norm_ffn[layer][None, :], mix_inputs, mix_params, wu, cw, cb, wd,
                        final_norm[None, :], geom_c, final=(layer == depth - 1), kind=kind)
        x = outs[0]

    y_prompt = outs[0].reshape(batch, seq, d)
    y_sample = outs[1].reshape(dec_batch, dec_seq, d)
    return (y_prompt, y_sample, jnp.stack(new_k, axis=1), jnp.stack(new_v, axis=1),
            jnp.stack(new_gla, axis=1), jnp.stack(new_rg, axis=1))
```
